```python
import math, functools
import jax, jax.numpy as jnp
from jax import lax
import numpy as np

D_MODEL = 1024
BATCH = 8
SEQ = 2048
DEPTH = 2
DEC_BATCH = 32
DEC_SEQ = 4
PAST_LEN = 16384
PAGE_SIZE = 128

HEAD_DIM = 64
ATTN_WIDTH = D_MODEL // 2
LIN_WIDTH = D_MODEL - ATTN_WIDTH
N_HEADS = ATTN_WIDTH // HEAD_DIM
N_KV_HEADS = N_HEADS // 2
GQA = N_HEADS // N_KV_HEADS
MOBA_BLOCK = 256
MOBA_TOPK = 3
Q_CHUNK = 32
N_BUCKETS = 32
MAX_DISTANCE = 128
H_LIN = LIN_WIDTH // HEAD_DIM
DK_LIN = HEAD_DIM
DV_LIN = HEAD_DIM
CONV_W = 4
CONV_CH = H_LIN * (2 * DK_LIN + DV_LIN)
GDN_CHUNK = 64
PEER_HEADS = 8
PEER_DK = 256
N_KEYS = 128
N_EXPERTS = N_KEYS * N_KEYS
PEER_TOPK = 16
TOK_CHUNK = 128
EPS = 1e-6
IN_SIZES = (N_HEADS * HEAD_DIM, N_KV_HEADS * HEAD_DIM, N_KV_HEADS * HEAD_DIM,
            H_LIN * DK_LIN, H_LIN * DK_LIN, H_LIN * DV_LIN, H_LIN * DV_LIN, H_LIN, H_LIN)
D_IN = sum(IN_SIZES)
IN_SPLITS = tuple(int(s) for s in np.cumsum(IN_SIZES)[:-1])

kernel_name = 'hymba_moba_gdn_peer_step'


def rms_norm(x, gain):
    x32 = x.astype(jnp.float32)
    y = x32 * lax.rsqrt(jnp.mean(x32 * x32, axis=-1, keepdims=True) + EPS)
    return (y * gain.astype(jnp.float32)).astype(x.dtype)


def l2_norm(x):
    x32 = x.astype(jnp.float32)
    return x32 * lax.rsqrt(jnp.sum(x32 * x32, axis=-1, keepdims=True) + EPS)


def t5_bucket(dist):
    n = jnp.maximum(dist, 0)
    max_exact = N_BUCKETS // 2
    nf = jnp.maximum(n, 1).astype(jnp.float32)
    large = max_exact + (jnp.log(nf / max_exact) / math.log(MAX_DISTANCE / max_exact)
                         * (N_BUCKETS - max_exact)).astype(jnp.int32)
    return jnp.where(n < max_exact, n, jnp.minimum(large, N_BUCKETS - 1))


def moba_combine(q, q_pos, sk, sv, s_pos, s_valid, ok, ov, o_pos, rel_bias):
    B, Q, KV, G, D = q.shape
    Lo = o_pos.shape[0]
    scale = D ** -0.5
    lo = jnp.einsum('bqkgd,bokd->bqkgo', q, ok).astype(jnp.float32) * scale
    ob = rel_bias[t5_bucket(q_pos[:, None] - o_pos[None, :])]
    ob = ob.reshape(Q, Lo, KV, G).transpose(0, 2, 3, 1).astype(jnp.float32)
    causal = (o_pos[None, :] <= q_pos[:, None])[:, None, None, :]
    lo = jnp.where(causal, lo + ob, -jnp.inf)
    if sk is None:
        p = jax.nn.softmax(lo, axis=-1).astype(ov.dtype)
        out = jnp.einsum('bqkgo,bokd->bqkgd', p, ov)
    else:
        Ls = sk.shape[4]
        head_ix = jnp.arange(KV * G).reshape(KV, G)
        ls = jnp.einsum('bqkgd,bqkgld->bqkgl', q, sk).astype(jnp.float32) * scale
        sb = rel_bias[t5_bucket(q_pos[None, :, None, None, None] - s_pos),
                      head_ix[None, None, :, :, None]].astype(jnp.float32)
        ls = jnp.where(s_valid, ls + sb, -jnp.inf)
        p = jax.nn.softmax(jnp.concatenate([ls, lo], axis=-1), axis=-1).astype(ov.dtype)
        out = (jnp.einsum('bqkgl,bqkgld->bqkgd', p[..., :Ls], sv)
               + jnp.einsum('bqkgo,bokd->bqkgd', p[..., Ls:], ov))
    return out.reshape(B, Q, KV * G * D)


def moba_prompt(q, k, v, rel_bias):
    B, T, KV, G, D = q.shape
    nb = -(-T // MOBA_BLOCK)
    tp = nb * MOBA_BLOCK
    kp = jnp.pad(k, ((0, 0), (0, tp - T), (0, 0), (0, 0)))
    vp = jnp.pad(v, ((0, 0), (0, tp - T), (0, 0), (0, 0)))
    kb = kp.reshape(B, nb, MOBA_BLOCK, KV, D).transpose(0, 3, 1, 2, 4)
    vb = vp.reshape(B, nb, MOBA_BLOCK, KV, D).transpose(0, 3, 1, 2, 4)
    kmean = jnp.mean(kb.astype(jnp.float32), axis=3)
    qblk = jnp.arange(T, dtype=jnp.int32) // MOBA_BLOCK
    gate = jnp.einsum('btkgd,bknd->btkgn', q.astype(jnp.float32), kmean)
    fully_past = jnp.arange(nb)[None, :] < qblk[:, None]
    gate = jnp.where(fully_past[None, :, None, None, :], gate, -jnp.inf)
    ksel = min(MOBA_TOPK, nb)
    _, sel = lax.top_k(gate, ksel)
    valid = jnp.arange(ksel)[None, :] < qblk[:, None]
    qc = min(Q_CHUNK, T)
    ls_len = ksel * MOBA_BLOCK
    b_ix = jnp.arange(B)[:, None, None, None, None]
    kv_ix = jnp.arange(KV)[None, None, :, None, None]
    offs = jnp.arange(MOBA_BLOCK, dtype=jnp.int32)

    def sweep(ci):
        q0 = ci * qc
        q_c = lax.dynamic_slice_in_dim(q, q0, qc, axis=1)
        sel_c = lax.dynamic_slice_in_dim(sel, q0, qc, axis=1)
        val_c = lax.dynamic_slice_in_dim(valid, q0, qc, axis=0)
        sk = kb[b_ix, kv_ix, sel_c].reshape(B, qc, KV, G, ls_len, D)
        sv = vb[b_ix, kv_ix, sel_c].reshape(B, qc, KV, G, ls_len, D)
        s_pos = (sel_c[..., None] * MOBA_BLOCK + offs).reshape(B, qc, KV, G, ls_len)
        s_valid = jnp.broadcast_to(val_c[None, :, None, None, :, None],
                                   (B, qc, KV, G, ksel, MOBA_BLOCK)).reshape(B, qc, KV, G, ls_len)
        start = (q0 // MOBA_BLOCK) * MOBA_BLOCK
        ok = lax.dynamic_slice_in_dim(kp, start, MOBA_BLOCK, axis=1)
        ov = lax.dynamic_slice_in_dim(vp, start, MOBA_BLOCK, axis=1)
        q_pos = q0 + jnp.arange(qc, dtype=jnp.int32)
        return moba_combine(q_c, q_pos, sk, sv, s_pos, s_valid, ok, ov, start + offs, rel_bias)

    out = lax.map(sweep, jnp.arange(T // qc, dtype=jnp.int32))
    return out.transpose(1, 0, 2, 3).reshape(B, T, KV * G * D)


def moba_sample(q, k, v, cache_k, cache_v, page_table, rel_bias):
    B, S, KV, G, D = q.shape
    n_pages = page_table.shape[1]
    past = n_pages * PAGE_SIZE
    n_full = past // MOBA_BLOCK
    own_start = n_full * MOBA_BLOCK
    k_past = cache_k[page_table].reshape(B, past, KV, D)
    v_own_past = cache_v[page_table[:, own_start // PAGE_SIZE:]].reshape(B, past - own_start, KV, D)
    ok = jnp.concatenate([k_past[:, own_start:].astype(k.dtype), k], axis=1)
    ov = jnp.concatenate([v_own_past.astype(v.dtype), v], axis=1)
    o_pos = own_start + jnp.arange(past - own_start + S, dtype=jnp.int32)
    q_pos = past + jnp.arange(S, dtype=jnp.int32)
    if n_full == 0:
        return moba_combine(q, q_pos, None, None, None, None, ok, ov, o_pos, rel_bias)
    kmean = k_past[:, :own_start].astype(jnp.float32).reshape(B, n_full, MOBA_BLOCK, KV, D).mean(axis=2)
    gate = jnp.einsum('bqkgd,bnkd->bqkgn', q.astype(jnp.float32), kmean)
    ksel = min(MOBA_TOPK, n_full)
    _, sel = lax.top_k(gate, ksel)
    ppb = MOBA_BLOCK // PAGE_SIZE
    logical = sel[..., None] * ppb + jnp.arange(ppb, dtype=jnp.int32)
    phys = page_table[jnp.arange(B)[:, None, None, None, None, None], logical]
    kv_ix = jnp.arange(KV)[None, None, :, None, None, None]
    ls_len = ksel * MOBA_BLOCK
    sk = cache_k[phys, :, kv_ix].reshape(B, S, KV, G, ls_len, D).astype(q.dtype)
    sv = cache_v[phys, :, kv_ix].reshape(B, S, KV, G, ls_len, D).astype(v.dtype)
    s_pos = (sel[..., None] * MOBA_BLOCK + jnp.arange(MOBA_BLOCK, dtype=jnp.int32)).reshape(B, S, KV, G, ls_len)
    s_valid = jnp.ones((B, S, KV, G, ls_len), dtype=bool)
    return moba_combine(q, q_pos, sk, sv, s_pos, s_valid, ok, ov, o_pos, rel_bias)


def gated_delta_rule(q, k, v, beta, g, s0):
    B, T, H, _ = q.shape
    C = min(GDN_CHUNK, T)
    n = -(-T // C)
    pad = n * C - T

    def to_chunks(a):
        a = jnp.moveaxis(a.astype(jnp.float32), 2, 1)
        a = jnp.pad(a, [(0, 0), (0, 0), (0, pad)] + [(0, 0)] * (a.ndim - 3))
        return a.reshape(a.shape[:2] + (n, C) + a.shape[3:])

    q, k, v, beta, g = to_chunks(q), to_chunks(k), to_chunks(v), to_chunks(beta), to_chunks(g)
    gc = jnp.cumsum(g, axis=-1)
    ii = jnp.arange(C)[:, None]
    jj = jnp.arange(C)[None, :]
    decay = jnp.exp(jnp.where(ii >= jj, gc[..., :, None] - gc[..., None, :], -jnp.inf))
    kb = k * beta[..., None]
    vb = v * beta[..., None]
    lmat = jnp.einsum('bhnid,bhnjd->bhnij', kb, k) * decay * (ii > jj)
    u = lax.linalg.triangular_solve(lmat, vb, left_side=True, lower=True, unit_diagonal=True)
    w = lax.linalg.triangular_solve(lmat, kb * jnp.exp(gc)[..., None], left_side=True, lower=True,
                                    unit_diagonal=True)
    qk = jnp.einsum('bhnid,bhnjd->bhnij', q, k) * decay
    qd = q * jnp.exp(gc)[..., None]
    kt = k * jnp.exp(gc[..., -1:] - gc)[..., None]
    glast = jnp.exp(gc[..., -1])

    def step(S, xs):
        u_n, w_n, qk_n, qd_n, kt_n, gl_n = xs
        v_new = u_n - jnp.einsum('bhcd,bhde->bhce', w_n, S)
        o = jnp.einsum('bhcd,bhde->bhce', qd_n, S) + jnp.einsum('bhij,bhje->bhie', qk_n, v_new)
        S = S * gl_n[..., None, None] + jnp.einsum('bhcd,bhce->bhde', kt_n, v_new)
        return S, o

    xs = tuple(jnp.moveaxis(a, 2, 0) for a in (u, w, qk, qd, kt, glast))
    S, o = lax.scan(step, s0.astype(jnp.float32), xs)
    o = o.transpose(1, 0, 3, 2, 4).reshape(B, n * C, H, -1)[:, :T]
    return o, S


def peer_ffn(h, wq, keys, u_tab, v_tab):
    B, T, D = h.shape
    q = (h @ wq).reshape(B, T, PEER_HEADS, 2, PEER_DK // 2)
    s = jnp.einsum('bthcd,hcnd->bthcn', q, keys).astype(jnp.float32)
    s1, i1 = lax.top_k(s[..., 0, :], PEER_TOPK)
    s2, i2 = lax.top_k(s[..., 1, :], PEER_TOPK)
    cand = (s1[..., :, None] + s2[..., None, :]).reshape(B, T, PEER_HEADS, PEER_TOPK * PEER_TOPK)
    cidx = (i1[..., :, None] * N_KEYS + i2[..., None, :]).reshape(B, T, PEER_HEADS, PEER_TOPK * PEER_TOPK)
    top_s, top_pos = lax.top_k(cand, PEER_TOPK)
    idx = jnp.take_along_axis(cidx, top_pos, axis=-1)
    gw = jax.nn.softmax(top_s, axis=-1).astype(h.dtype)
    N = B * T
    E = PEER_HEADS * PEER_TOPK
    C = TOK_CHUNK
    nc = -(-N // C)
    pad = nc * C - N
    hf = jnp.pad(h.reshape(N, D), ((0, pad), (0, 0))).reshape(nc, C, D)
    idf = jnp.pad(idx.reshape(N, E), ((0, pad), (0, 0))).reshape(nc, C, E)
    gwf = jnp.pad(gw.reshape(N, E), ((0, pad), (0, 0))).reshape(nc, C, E)

    def retrieve(args):
        hc, ic, gc = args
        act = jax.nn.gelu(jnp.einsum('ced,cd->ce', u_tab[ic], hc), approximate=False)
        return jnp.einsum('ce,ced->cd', gc * act, v_tab[ic])

    out = lax.map(retrieve, (hf, idf, gwf))
    return out.reshape(nc * C, D)[:N].reshape(B, T, D)


def trunk_layer(x, c, attend, conv_buf, s0, p):
    B, T, _ = x.shape
    mod = (jax.nn.silu(c) @ p['w_ada'] + p['b_ada'])[:, None, :]
    sh_a, sc_a, g_a, sh_f, sc_f, g_f = jnp.split(mod, 6, axis=-1)
    h = rms_norm(x, p['norm_attn']) * (1 + sc_a) + sh_a
    proj = h @ p['w_in']
    aq, ak, av, lq, lk, lv, lz, lb, la = jnp.split(proj, IN_SPLITS, axis=-1)
    aq = rms_norm(aq.reshape(B, T, N_KV_HEADS, GQA, HEAD_DIM), p['q_norm'])
    ak = rms_norm(ak.reshape(B, T, N_KV_HEADS, HEAD_DIM), p['k_norm'])
    av = av.reshape(B, T, N_KV_HEADS, HEAD_DIM)
    o_attn = attend(aq, ak, av)
    qkv = jnp.concatenate([lq, lk, lv], axis=-1)
    xpad = jnp.concatenate([conv_buf.astype(qkv.dtype), qkv], axis=1)
    conv_new = xpad[:, -(CONV_W - 1):]
    conv = p['conv_w'][0] * xpad[:, 0:T]
    for i in range(1, CONV_W):
        conv = conv + p['conv_w'][i] * xpad[:, i:i + T]
    qkv = jax.nn.silu(conv)
    lq, lk, lv = jnp.split(qkv, [H_LIN * DK_LIN, 2 * H_LIN * DK_LIN], axis=-1)
    lq = l2_norm(lq.reshape(B, T, H_LIN, DK_LIN)) * DK_LIN ** -0.5
    lk = l2_norm(lk.reshape(B, T, H_LIN, DK_LIN))
    lv = lv.reshape(B, T, H_LIN, DV_LIN)
    beta = jax.nn.sigmoid(lb.astype(jnp.float32))
    g = -jnp.exp(p['a_log'].astype(jnp.float32)) * jax.nn.softplus(
        la.astype(jnp.float32) + p['dt_bias'].astype(jnp.float32))
    o_lin, s_new = gated_delta_rule(lq, lk, lv, beta, g, s0)
    o_lin = rms_norm(o_lin.astype(x.dtype), p['o_norm']) * jax.nn.silu(lz.reshape(B, T, H_LIN, DV_LIN))
    y = jnp.concatenate([o_attn, o_lin.reshape(B, T, LIN_WIDTH)], axis=-1) @ p['w_out']
    x = x + g_a * y
    h2 = rms_norm(x, p['norm_ffn']) * (1 + sc_f) + sh_f
    x = x + g_f * peer_ffn(h2, p['peer_wq'], p['peer_keys'], p['peer_u'], p['peer_v'])
    return x, ak, av, s_new.astype(s0.dtype), conv_new.astype(conv_buf.dtype)


def setup_inputs(seed: int = 0) -> dict:
    key = jax.random.key(seed)
    ks = jax.random.split(key, 32)
    n_pages = PAST_LEN // PAGE_SIZE
    n_used = DEC_BATCH * n_pages
    n_pool = (5 * n_used + 3) // 4

    def nrm(k, shape, s):
        return jax.random.normal(k, shape, jnp.float32) * s

    page_table = jax.random.permutation(ks[0], n_pool)[:n_used].reshape(DEC_BATCH, n_pages).astype(jnp.int32)
    return {
        'x_prompt': nrm(ks[1], (BATCH, SEQ, D_MODEL), 1.0),
        'x_sample': nrm(ks[2], (DEC_BATCH, DEC_SEQ, D_MODEL), 1.0),
        'cache_k': nrm(ks[3], (DEPTH, n_pool, PAGE_SIZE, N_KV_HEADS, HEAD_DIM), 1.0),
        'cache_v': nrm(ks[4], (DEPTH, n_pool, PAGE_SIZE, N_KV_HEADS, HEAD_DIM), 1.0),
        'state_ssm': nrm(ks[5], (DEPTH, DEC_BATCH, H_LIN, DK_LIN, DV_LIN), 0.1),
        'state_conv': nrm(ks[6], (DEPTH, DEC_BATCH, CONV_W - 1, CONV_CH), 1.0),
        'page_table': page_table,
        'c_prompt': nrm(ks[7], (BATCH, D_MODEL), 1.0),
        'c_sample': nrm(ks[8], (DEC_BATCH, D_MODEL), 1.0),
        'rel_bias': nrm(ks[9], (N_BUCKETS, N_HEADS), 0.1),
        'w_ada': nrm(ks[10], (DEPTH, D_MODEL, 6 * D_MODEL), 0.5 * D_MODEL ** -0.5),
        'b_ada': nrm(ks[11], (DEPTH, 6 * D_MODEL), 0.02),
        'norm_attn': 1.0 + nrm(ks[12], (DEPTH, D_MODEL), 0.02),
        'norm_ffn': 1.0 + nrm(ks[13], (DEPTH, D_MODEL), 0.02),
        'w_in': nrm(ks[14], (DEPTH, D_MODEL, D_IN), D_MODEL ** -0.5),
        'w_out': nrm(ks[15], (DEPTH, D_MODEL, D_MODEL), D_MODEL ** -0.5),
        'q_norm': 1.0 + nrm(ks[16], (DEPTH, HEAD_DIM), 0.02),
        'k_norm': 1.0 + nrm(ks[17], (DEPTH, HEAD_DIM), 0.02),
        'conv_w': nrm(ks[18], (DEPTH, CONV_W, CONV_CH), CONV_W ** -0.5),
        'a_log': jnp.log(jax.random.uniform(ks[19], (DEPTH, H_LIN), jnp.float32, 1.0, 16.0)),
        'dt_bias': nrm(ks[20], (DEPTH, H_LIN), 0.1),
        'o_norm': 1.0 + nrm(ks[21], (DEPTH, DV_LIN), 0.02),
        'peer_wq': nrm(ks[22], (DEPTH, D_MODEL, PEER_HEADS * PEER_DK), D_MODEL ** -0.5),
        'peer_keys': nrm(ks[23], (DEPTH, PEER_HEADS, 2, N_KEYS, PEER_DK // 2), (PEER_DK // 2) ** -0.5),
        'peer_u': nrm(ks[24], (DEPTH, N_EXPERTS, D_MODEL), D_MODEL ** -0.5),
        'peer_v': nrm(ks[25], (DEPTH, N_EXPERTS, D_MODEL), 0.5),
    }


def reference(x_prompt, x_sample, cache_k, cache_v, state_ssm, state_conv, page_table, c_prompt, c_sample,
              rel_bias, w_ada, b_ada, norm_attn, norm_ffn, w_in, w_out, q_norm, k_norm, conv_w, a_log,
              dt_bias, o_norm, peer_wq, peer_keys, peer_u, peer_v):
    B = x_prompt.shape[0]
    xp, xs = x_prompt, x_sample
    kp_l, vp_l, ks_l, vs_l, sp_l, ss_l, cp_l, cs_l = [], [], [], [], [], [], [], []
    for l in range(DEPTH):
        p = {'w_ada': w_ada[l], 'b_ada': b_ada[l], 'norm_attn': norm_attn[l], 'norm_ffn': norm_ffn[l],
             'w_in': w_in[l], 'w_out': w_out[l], 'q_norm': q_norm[l], 'k_norm': k_norm[l],
             'conv_w': conv_w[l], 'a_log': a_log[l], 'dt_bias': dt_bias[l], 'o_norm': o_norm[l],
             'peer_wq': peer_wq[l], 'peer_keys': peer_keys[l], 'peer_u': peer_u[l], 'peer_v': peer_v[l]}
        attend_p = functools.partial(moba_prompt, rel_bias=rel_bias)
        attend_s = functools.partial(moba_sample, cache_k=cache_k[l], cache_v=cache_v[l],
                                     page_table=page_table, rel_bias=rel_bias)
        conv0 = jnp.zeros((B, CONV_W - 1, CONV_CH), xp.dtype)
        ssm0 = jnp.zeros((B, H_LIN, DK_LIN, DV_LIN), state_ssm.dtype)
        xp, kp, vp, sp, cp = trunk_layer(xp, c_prompt, attend_p, conv0, ssm0, p)
        xs, ks, vs, ss, cs = trunk_layer(xs, c_sample, attend_s, state_conv[l], state_ssm[l], p)
        kp_l.append(kp); vp_l.append(vp); ks_l.append(ks); vs_l.append(vs)
        sp_l.append(sp); ss_l.append(ss); cp_l.append(cp); cs_l.append(cs)
    return (xp, xs, jnp.stack(kp_l), jnp.stack(vp_l), jnp.stack(ks_l), jnp.stack(vs_l),
            jnp.stack(sp_l), jnp.stack(ss_l), jnp.stack(cp_l), jnp.stack(cs_l))
```

```python
import functools
import math

import jax
import jax.numpy as jnp
from jax import lax
from jax.experimental import pallas as pl
from jax.experimental.pallas import tpu as pltpu

F32 = jnp.float32
BF16 = jnp.bfloat16
I32 = jnp.int32

HEAD_DIM = 64
N_HEADS = 8
N_KV_HEADS = 4
GQA = N_HEADS // N_KV_HEADS
ATTN_W = N_HEADS * HEAD_DIM
KV_W = N_KV_HEADS * HEAD_DIM
MOBA_BLOCK = 256
MOBA_TOPK = 3
N_BUCKETS = 32
MAX_DISTANCE = 128
H_LIN = 8
LIN_W = H_LIN * HEAD_DIM
CONV_W = 4
CONV_CH = 3 * LIN_W
GDN_CHUNK = 64
INV_BLOCK = 16
PEER_HEADS = 8
N_KEYS = 128
PEER_TOPK = 16
PAGE_SIZE = 128
EPS = 1e-6

LANES = 128
SUBLANES = 8
VMEM_LIMIT_BYTES = 56 * 1024 * 1024

D_IN_PAD = ATTN_W + 2 * KV_W + CONV_CH + LIN_W + LANES
NEG_INF = float("-inf")


def _cparams(n_axes):
    return pltpu.CompilerParams(dimension_semantics=("arbitrary",) * n_axes,
                                vmem_limit_bytes=VMEM_LIMIT_BYTES)


def _dot(a, b):
    return jnp.dot(a.astype(BF16), b.astype(BF16), preferred_element_type=F32)


def _dot_nt(a, b):
    return lax.dot_general(a.astype(BF16), b.astype(BF16), (((1,), (1,)), ((), ())),
                           preferred_element_type=F32)


def _dot_tn(a, b):
    return lax.dot_general(a.astype(BF16), b.astype(BF16), (((0,), (0,)), ((), ())),
                           preferred_element_type=F32)


def _split2(a):
    hi = a.astype(BF16)
    lo = (a - hi.astype(F32)).astype(BF16)
    return hi, lo


def _split3(a):
    hi = a.astype(BF16)
    r = a - hi.astype(F32)
    mid = r.astype(BF16)
    lo = (r - mid.astype(F32)).astype(BF16)
    return hi, mid, lo


def _dot3(a, b):
    ah, al = _split2(a)
    bh, bl = _split2(b)
    d = functools.partial(jnp.dot, preferred_element_type=F32)
    return d(ah, bh) + (d(al, bh) + d(ah, bl))


def _dot3_nt(a, b):
    ah, al = _split2(a)
    bh, bl = _split2(b)
    d = functools.partial(lax.dot_general, dimension_numbers=(((1,), (1,)), ((), ())),
                          preferred_element_type=F32)
    return d(ah, bh) + (d(al, bh) + d(ah, bl))


def _dot_mask_rhs(a, mask_bf16):
    hi, mid, lo = _split3(a)
    d = functools.partial(jnp.dot, preferred_element_type=F32)
    return d(hi, mask_bf16) + (d(mid, mask_bf16) + d(lo, mask_bf16))


def _dot_mask_lhs(mask_bf16, b):
    hi, mid, lo = _split3(b)
    d = functools.partial(jnp.dot, preferred_element_type=F32)
    return d(mask_bf16, hi) + (d(mask_bf16, mid) + d(mask_bf16, lo))


def _sigmoid(x):
    return 1.0 / (1.0 + jnp.exp(-x))


def _softplus(x):
    return jnp.maximum(x, 0.0) + jnp.log(1.0 + jnp.exp(-jnp.abs(x)))


def _head_sumsq(a, bd):
    return _dot_mask_rhs(a * a, bd)


def _ada_body(c_ref, w_ref, b_ref, o_ref):
    c = c_ref[...]
    s = c * _sigmoid(c)
    o_ref[0] = _dot3(s, w_ref[0]) + b_ref[0]


def _ada(c, w_ada, b_ada):
    n_l, d, d6 = w_ada.shape
    r = c.shape[0]
    tn = 512
    return pl.pallas_call(
        _ada_body,
        grid=(n_l, d6 // tn),
        in_specs=[pl.BlockSpec((r, d), lambda l, j: (0, 0)),
                  pl.BlockSpec((1, d, tn), lambda l, j: (l, 0, j)),
                  pl.BlockSpec((1, 1, tn), lambda l, j: (l, 0, j))],
        out_specs=pl.BlockSpec((1, r, tn), lambda l, j: (l, 0, j)),
        out_shape=jax.ShapeDtypeStruct((n_l, r, d6), F32),
        compiler_params=_cparams(2),
        name="ada",
    )(c, w_ada, b_ada.reshape(n_l, 1, d6))


def _inproj_body(x_ref, sc_ref, sh_ref, gn_ref, w_ref, qg_ref, kg_ref, bd_ref,
                 q_ref, k_ref, v_ref, lin_ref, z_ref, ba_ref, km_ref):
    x = x_ref[...]
    ms = jnp.mean(x * x, axis=-1, keepdims=True)
    h = x * lax.rsqrt(ms + EPS) * gn_ref[...]
    h = h * (1.0 + sc_ref[0]) + sh_ref[0]
    proj = jnp.dot(h.astype(BF16), w_ref[...], preferred_element_type=F32)
    bd = bd_ref[...]
    aq = proj[:, :ATTN_W]
    ak = proj[:, ATTN_W:ATTN_W + KV_W]
    q = aq * lax.rsqrt(_head_sumsq(aq, bd) * (1.0 / HEAD_DIM) + EPS) * qg_ref[...]
    k = ak * lax.rsqrt(_head_sumsq(ak, bd[:KV_W, :KV_W]) * (1.0 / HEAD_DIM) + EPS) * kg_ref[...]
    q_ref[...] = q * (HEAD_DIM ** -0.5)
    k_ref[...] = k
    o = ATTN_W + KV_W
    v_ref[...] = proj[:, o:o + KV_W]
    o += KV_W
    lin_ref[...] = proj[:, o:o + CONV_CH]
    o += CONV_CH
    z_ref[...] = proj[:, o:o + LIN_W]
    o += LIN_W
    ba_ref[...] = proj[:, o:o + LANES]
    km_ref[0] = jnp.mean(k, axis=0, keepdims=True)


def _inproj(x, sc, sh, gn, w_pad, qg, kg, bd, tm, tiles_per_mod):
    n, d = x.shape
    r = sc.shape[1]
    nt = n // tm
    mod_spec = pl.BlockSpec((1, r, d), lambda i: (i // tiles_per_mod, 0, 0))
    const = lambda shape: pl.BlockSpec(shape, lambda i: (0,) * len(shape))
    row = lambda w: pl.BlockSpec((tm, w), lambda i: (i, 0))
    outs = pl.pallas_call(
        _inproj_body,
        grid=(nt,),
        in_specs=[row(d), mod_spec, mod_spec, const((1, d)), const(w_pad.shape),
                  const((1, ATTN_W)), const((1, KV_W)), const((ATTN_W, ATTN_W))],
        out_specs=[row(ATTN_W), row(KV_W), row(KV_W), row(CONV_CH), row(LIN_W), row(LANES),
                   pl.BlockSpec((1, 1, KV_W), lambda i: (i, 0, 0))],
        out_shape=[jax.ShapeDtypeStruct((n, ATTN_W), F32), jax.ShapeDtypeStruct((n, KV_W), F32),
                   jax.ShapeDtypeStruct((n, KV_W), F32), jax.ShapeDtypeStruct((n, CONV_CH), F32),
                   jax.ShapeDtypeStruct((n, LIN_W), F32), jax.ShapeDtypeStruct((n, LANES), F32),
                   jax.ShapeDtypeStruct((nt, 1, KV_W), F32)],
        compiler_params=_cparams(1),
        name="inproj",
    )(x, sc, sh, gn, w_pad, qg, kg, bd)
    return outs


def _relbias_body(rb_ref, o_ref):
    n_tab, n_h, r, c = o_ref.shape
    row = lax.broadcasted_iota(I32, (r, c), 0)
    col = lax.broadcasted_iota(I32, (r, c), 1)
    max_exact = N_BUCKETS // 2
    for t in range(n_tab):
        dist = jnp.maximum(row - col + t * MOBA_BLOCK, 0)
        nf = jnp.maximum(dist, 1).astype(F32)
        large = max_exact + (jnp.log(nf / max_exact) / math.log(MAX_DISTANCE / max_exact)
                             * (N_BUCKETS - max_exact)).astype(I32)
        bucket = jnp.where(dist < max_exact, dist, jnp.minimum(large, N_BUCKETS - 1))

        def per_head(h, _, bucket=bucket, t=t):
            acc = jnp.zeros((r, c), F32)
            for j in range(N_BUCKETS):
                acc = jnp.where(bucket == j, rb_ref[j * n_h + h], acc)
            o_ref[t, h] = acc
            return 0

        lax.fori_loop(0, n_h, per_head, 0)


def _relbias(rel_bias):
    n_h = rel_bias.shape[1]
    return pl.pallas_call(
        _relbias_body,
        in_specs=[pl.BlockSpec(memory_space=pltpu.SMEM)],
        out_specs=pl.BlockSpec(memory_space=pltpu.VMEM),
        out_shape=jax.ShapeDtypeStruct((2, n_h, MOBA_BLOCK, MOBA_BLOCK), F32),
        compiler_params=pltpu.CompilerParams(vmem_limit_bytes=VMEM_LIMIT_BYTES),
        name="relbias",
    )(rel_bias.reshape(-1))


def _block_rank(g, idx, n):
    rank = jnp.zeros(g.shape, F32)
    lane_axis = idx.ndim - 1
    for m in range(n):
        gm = lax.slice_in_dim(g, m, m + 1, axis=lane_axis)
        beats = (gm > g) | ((gm == g) & (idx > m))
        rank = rank + jnp.where(beats, 1.0, 0.0)
    return rank


def _moba_prompt_body(rb_ref, q_ref, km_ref, k_ref, v_ref, bown_ref, bprev_ref, o_ref,
                      selb, m_s, l_s, acc):
    i = pl.program_id(1)
    n = pl.program_id(2)
    nb = km_ref.shape[1]
    blk = q_ref.shape[1]
    far_bucket = N_BUCKETS - 1

    def qh(h):
        return q_ref[0, :, h * HEAD_DIM:(h + 1) * HEAD_DIM]

    def kh(h):
        kv = h // GQA
        return k_ref[0, :, kv * HEAD_DIM:(kv + 1) * HEAD_DIM]

    def vh(h):
        kv = h // GQA
        return v_ref[0, :, kv * HEAD_DIM:(kv + 1) * HEAD_DIM]

    @pl.when(n == 0)
    def _first():
        col = lax.broadcasted_iota(I32, (blk, nb), 1)
        r2 = lax.broadcasted_iota(I32, (blk, blk), 0)
        c2 = lax.broadcasted_iota(I32, (blk, blk), 1)
        causal = c2 <= r2
        for h in range(N_HEADS):
            kv = h // GQA
            km = km_ref[0, :, kv * HEAD_DIM:(kv + 1) * HEAD_DIM]
            gate = _dot3_nt(qh(h), km)
            g = jnp.where(col < i, gate, NEG_INF)
            rank = _block_rank(g, col, nb)
            sel = jnp.where((rank < MOBA_TOPK) & (col < i), 1.0, 0.0)
            for nn in range(nb):
                selb[h, nn] = jnp.broadcast_to(sel[:, nn:nn + 1], (blk, LANES))
            s = _dot_nt(qh(h), kh(h)) + bown_ref[h]
            s = jnp.where(causal, s, NEG_INF)
            m = jnp.max(s, axis=-1, keepdims=True)
            p = jnp.exp(s - m)
            m_s[h] = m
            l_s[h] = jnp.sum(p, axis=-1, keepdims=True)
            acc[h] = _dot(p, vh(h))

    @pl.when((n >= 1) & (n <= i))
    def _past():
        kb = n - 1
        is_prev = kb == i - 1
        for h in range(N_HEADS):
            bias = jnp.where(is_prev, bprev_ref[h], rb_ref[far_bucket * N_HEADS + h])
            s = _dot_nt(qh(h), kh(h)) + bias
            sel = jnp.concatenate([selb[h, kb]] * (blk // LANES), axis=1)
            s = jnp.where(sel > 0.5, s, NEG_INF)
            m_old = m_s[h]
            m_new = jnp.maximum(m_old, jnp.max(s, axis=-1, keepdims=True))
            alpha = jnp.exp(m_old - m_new)
            p = jnp.exp(s - m_new)
            m_s[h] = m_new
            l_s[h] = alpha * l_s[h] + jnp.sum(p, axis=-1, keepdims=True)
            acc[h] = alpha * acc[h] + _dot(p, vh(h))

    @pl.when(n == nb - 1)
    def _done():
        for h in range(N_HEADS):
            o_ref[0, :, h * HEAD_DIM:(h + 1) * HEAD_DIM] = acc[h] / l_s[h]


def _moba_prompt(rb_flat, q, k, v, kmean, btab):
    b, t, _ = q.shape
    nb = t // MOBA_BLOCK
    blk = MOBA_BLOCK

    def kv_map(bi, i, n):
        past = jnp.clip(n - 1, 0, jnp.maximum(i - 1, 0))
        return (bi, jnp.where(n == 0, i, past), 0)

    return pl.pallas_call(
        _moba_prompt_body,
        grid=(b, nb, nb),
        in_specs=[pl.BlockSpec(memory_space=pltpu.SMEM),
                  pl.BlockSpec((1, blk, ATTN_W), lambda bi, i, n: (bi, i, 0)),
                  pl.BlockSpec((1, nb, KV_W), lambda bi, i, n: (bi, 0, 0)),
                  pl.BlockSpec((1, blk, KV_W), kv_map),
                  pl.BlockSpec((1, blk, KV_W), kv_map),
                  pl.BlockSpec((None, N_HEADS, blk, blk), lambda bi, i, n: (0, 0, 0, 0)),
                  pl.BlockSpec((None, N_HEADS, blk, blk), lambda bi, i, n: (1, 0, 0, 0))],
        out_specs=pl.BlockSpec((1, blk, ATTN_W), lambda bi, i, n: (bi, i, 0)),
        out_shape=jax.ShapeDtypeStruct((b, t, ATTN_W), F32),
        scratch_shapes=[pltpu.VMEM((N_HEADS, nb, blk, LANES), F32),
                        pltpu.VMEM((N_HEADS, blk, 1), F32),
                        pltpu.VMEM((N_HEADS, blk, 1), F32),
                        pltpu.VMEM((N_HEADS, blk, HEAD_DIM), F32)],
        compiler_params=_cparams(3),
        name="moba_prompt",
    )(rb_flat, q, kmean, k, v, btab, btab)


PAGES_PER_STEP = 16
PAGES_PER_BLOCK = MOBA_BLOCK // PAGE_SIZE
BLOCKS_PER_STEP = PAGES_PER_STEP // PAGES_PER_BLOCK


def _page_specs(n_pages):
    def spec(r):
        return pl.BlockSpec((1, PAGE_SIZE, KV_W),
                            lambda b, j, pt: (pt[b * n_pages + j * PAGES_PER_STEP + r], 0, 0))
    return [spec(r) for r in range(PAGES_PER_STEP)]


def _kmean_pages_body(pt_ref, *refs):
    pages = refs[:PAGES_PER_STEP]
    o_ref = refs[PAGES_PER_STEP]
    for r in range(BLOCKS_PER_STEP):
        s = jnp.zeros((1, KV_W), F32)
        for p in range(PAGES_PER_BLOCK):
            s = s + jnp.sum(pages[r * PAGES_PER_BLOCK + p][0], axis=0, keepdims=True)
        o_ref[0, r:r + 1, :] = s * (1.0 / MOBA_BLOCK)


def _kmean_pages(pt_flat, cache_flat, b2, n_pages):
    n_blocks = n_pages // PAGES_PER_BLOCK
    return pl.pallas_call(
        _kmean_pages_body,
        grid_spec=pltpu.PrefetchScalarGridSpec(
            num_scalar_prefetch=1,
            grid=(b2, n_pages // PAGES_PER_STEP),
            in_specs=_page_specs(n_pages),
            out_specs=pl.BlockSpec((1, BLOCKS_PER_STEP, KV_W), lambda b, j, pt: (b, j, 0))),
        out_shape=jax.ShapeDtypeStruct((b2, n_blocks, KV_W), F32),
        compiler_params=_cparams(2),
        name="kmean_pages",
    )(pt_flat, *([cache_flat] * PAGES_PER_STEP))


def _moba_sample_body(pt_ref, qw_ref, km_ref, kn_ref, vn_ref, bl_ref, bo_ref, far_ref, *refs):
    kp = refs[:PAGES_PER_STEP]
    vp = refs[PAGES_PER_STEP:2 * PAGES_PER_STEP]
    o_ref, selc, m_s, l_s, acc = refs[2 * PAGES_PER_STEP:]
    j = pl.program_id(1)
    n_steps = pl.num_programs(1)
    n_blocks = km_ref.shape[1]
    rows = qw_ref.shape[1]
    qw = qw_ref[0]
    qw_b = qw.astype(BF16)

    @pl.when(j == 0)
    def _first():
        gate = _dot3_nt(qw, km_ref[0])
        col = lax.broadcasted_iota(I32, (rows, n_blocks), 1)
        rank = _block_rank(gate, col, n_blocks)
        sel = jnp.where(rank < MOBA_TOPK, 1.0, 0.0)
        for jj in range(n_blocks // BLOCKS_PER_STEP):
            selc[jj] = sel[:, jj * BLOCKS_PER_STEP:(jj + 1) * BLOCKS_PER_STEP]
        s = _dot_nt(qw_b, kn_ref[0]) + bo_ref[...]
        m = jnp.max(s, axis=-1, keepdims=True)
        p = jnp.exp(s - m)
        m_s[...] = m
        l_s[...] = jnp.sum(p, axis=-1, keepdims=True)
        acc[...] = _dot(p, vn_ref[0])

    sel_j = selc[j]
    for nn in range(BLOCKS_PER_STEP):
        kb = jnp.concatenate([kp[nn * PAGES_PER_BLOCK + p][0] for p in range(PAGES_PER_BLOCK)], axis=0)
        vb = jnp.concatenate([vp[nn * PAGES_PER_BLOCK + p][0] for p in range(PAGES_PER_BLOCK)], axis=0)
        s = _dot_nt(qw_b, kb)
        if nn == BLOCKS_PER_STEP - 1:
            s = s + jnp.where(j == n_steps - 1, bl_ref[...], far_ref[...])
        else:
            s = s + far_ref[...]
        s = jnp.where(sel_j[:, nn:nn + 1] > 0.5, s, NEG_INF)
        m_old = m_s[...]
        m_new = jnp.maximum(m_old, jnp.max(s, axis=-1, keepdims=True))
        alpha = jnp.exp(m_old - m_new)
        p = jnp.exp(s - m_new)
        m_s[...] = m_new
        l_s[...] = alpha * l_s[...] + jnp.sum(p, axis=-1, keepdims=True)
        acc[...] = alpha * acc[...] + _dot(p, vb)

    @pl.when(j == n_steps - 1)
    def _done():
        out = acc[...] / l_s[...]
        rpk = rows // N_KV_HEADS
        for kv in range(N_KV_HEADS):
            o_ref[0, kv * rpk:(kv + 1) * rpk, :] = out[kv * rpk:(kv + 1) * rpk,
                                                      kv * HEAD_DIM:(kv + 1) * HEAD_DIM]


def _moba_sample(pt_flat, ck_flat, cv_flat, qw, kmean, kn, vn, bias_last, bias_own, far, n_pages):
    b2, rows, _ = qw.shape
    n_blocks = kmean.shape[1]
    n_steps = n_pages // PAGES_PER_STEP
    s_pad = kn.shape[1]
    per_b = lambda shape: pl.BlockSpec((1,) + shape, lambda b, j, pt: (b, 0, 0))
    const = lambda shape: pl.BlockSpec(shape, lambda b, j, pt: (0, 0))
    return pl.pallas_call(
        _moba_sample_body,
        grid_spec=pltpu.PrefetchScalarGridSpec(
            num_scalar_prefetch=1,
            grid=(b2, n_steps),
            in_specs=[per_b((rows, KV_W)), per_b((n_blocks, KV_W)), per_b((s_pad, KV_W)),
                      per_b((s_pad, KV_W)), const((rows, MOBA_BLOCK)), const((rows, s_pad)),
                      const((rows, 1))] + _page_specs(n_pages) + _page_specs(n_pages),
            out_specs=pl.BlockSpec((1, rows, HEAD_DIM), lambda b, j, pt: (b, 0, 0)),
            scratch_shapes=[pltpu.VMEM((n_steps, rows, BLOCKS_PER_STEP), F32),
                            pltpu.VMEM((rows, 1), F32),
                            pltpu.VMEM((rows, 1), F32),
                            pltpu.VMEM((rows, KV_W), F32)]),
        out_shape=jax.ShapeDtypeStruct((b2, rows, HEAD_DIM), F32),
        compiler_params=_cparams(2),
        name="moba_sample",
    )(pt_flat, qw, kmean, kn, vn, bias_last, bias_own, far,
      *([ck_flat] * PAGES_PER_STEP), *([cv_flat] * PAGES_PER_STEP))


def _gdn_body(lin_ref, z_ref, ba_ref, cw_ref, cb_ref, s0_ref, al_ref, dtb_ref, og_ref, bd_ref,
              o_ref, sout_ref, cout_ref, xbuf, state, *, t_valid):
    t = pl.program_id(1)
    n_t = pl.num_programs(1)
    c = GDN_CHUNK
    halo = SUBLANES

    @pl.when(t == 0)
    def _init():
        xbuf[0:halo, :] = cb_ref[0]
        state[...] = s0_ref[0]

    @pl.when(t > 0)
    def _carry():
        xbuf[0:halo, :] = xbuf[c:c + halo, :]

    xbuf[halo:halo + c, :] = lin_ref[0]
    first = halo - (CONV_W - 1)
    conv = cw_ref[0:1, :] * xbuf[pl.ds(first, c), :]
    for i in range(1, CONV_W):
        conv = conv + cw_ref[i:i + 1, :] * xbuf[pl.ds(first + i, c), :]
    a = conv * _sigmoid(conv)
    lq = a[:, :LIN_W]
    lk = a[:, LIN_W:2 * LIN_W]
    lv = a[:, 2 * LIN_W:]
    bd = bd_ref[...]
    lq = lq * lax.rsqrt(_head_sumsq(lq, bd) + EPS) * (HEAD_DIM ** -0.5)
    lk = lk * lax.rsqrt(_head_sumsq(lk, bd) + EPS)
    ba = ba_ref[0]
    beta_all = _sigmoid(ba)
    g_all = -jnp.exp(al_ref[...]) * _softplus(ba + dtb_ref[...])
    if t_valid % c != 0:
        rid = t * c + lax.broadcasted_iota(I32, (c, 1), 0)
        ok = rid < t_valid
        lq = jnp.where(ok, lq, 0.0)
        lk = jnp.where(ok, lk, 0.0)
        lv = jnp.where(ok, lv, 0.0)
        beta_all = jnp.where(ok, beta_all, 0.0)
        g_all = jnp.where(ok, g_all, 0.0)

    ii = lax.broadcasted_iota(I32, (c, c), 0)
    jj = lax.broadcasted_iota(I32, (c, c), 1)
    tril = jnp.where(ii >= jj, 1.0, 0.0).astype(BF16)
    eye = jnp.where(ii == jj, 1.0, 0.0)
    lower = ii >= jj
    strict = ii > jj
    diag_blk = (ii // INV_BLOCK) == (jj // INV_BLOCK)
    gc_all = _dot_mask_lhs(tril, g_all)

    for h in range(H_LIN):
        sl = slice(h * HEAD_DIM, (h + 1) * HEAD_DIM)
        q, k, v = lq[:, sl], lk[:, sl], lv[:, sl]
        beta = beta_all[:, h:h + 1]
        gc = gc_all[:, H_LIN + h:H_LIN + h + 1]
        gc_row = jnp.sum(jnp.where(ii == jj, gc, 0.0), axis=0, keepdims=True)
        eg = jnp.exp(gc)
        dmat = jnp.where(lower, jnp.exp(jnp.minimum(gc - gc_row, 0.0)), 0.0)
        kb = k * beta
        vb = v * beta
        lm = _dot_nt(kb, k) * jnp.where(strict, dmat, 0.0)
        dm = jnp.where(diag_blk, lm, 0.0)
        nm = lm - dm
        x = eye - dm
        p = _dot3(dm, dm)
        steps = int(math.log2(INV_BLOCK)) - 1
        for s in range(steps):
            x = x + _dot3(x, p)
            if s + 1 < steps:
                p = _dot3(p, p)
        mm = _dot3(x, nm)
        n_blk = c // INV_BLOCK
        y = eye - mm
        pm = _dot3(mm, mm)
        msteps = int(math.log2(n_blk)) - 1
        for s in range(msteps):
            y = y + _dot3(y, pm)
            if s + 1 < msteps:
                pm = _dot3(pm, pm)
        tinv = _dot3(y, x)
        u = _dot(tinv, vb)
        w = _dot(tinv, kb * eg)
        qk = _dot_nt(q, k) * dmat
        qd = q * eg
        gl = gc[c - 1:c, :]
        kt = k * jnp.exp(gl - gc)
        s_h = state[h]
        v_new = u - _dot(w, s_h)
        o = _dot(qd, s_h) + _dot(qk, v_new)
        state[h] = s_h * jnp.exp(gl) + _dot_tn(kt, v_new)
        on = o * lax.rsqrt(jnp.mean(o * o, axis=-1, keepdims=True) + EPS) * og_ref[...]
        zh = z_ref[0, :, sl]
        o_ref[0, :, sl] = on * (zh * _sigmoid(zh))

    @pl.when(t == n_t - 1)
    def _fin():
        sout_ref[0] = state[...]
        tv = t_valid - (t_valid - 1) // c * c
        cout_ref[0] = xbuf[pl.ds(halo + tv - (CONV_W - 1), CONV_W - 1), :]


def _gdn(lin, z, ba, conv_w, cbuf8, s0, alog_l, dtb_l, og, bd, t_valid):
    b, tp, _ = lin.shape
    c = GDN_CHUNK
    n_t = tp // c
    tile = lambda w: pl.BlockSpec((1, c, w), lambda bi, t: (bi, t, 0))
    const = lambda shape: pl.BlockSpec(shape, lambda bi, t: (0,) * len(shape))
    return pl.pallas_call(
        functools.partial(_gdn_body, t_valid=t_valid),
        grid=(b, n_t),
        in_specs=[tile(CONV_CH), tile(LIN_W), tile(LANES), const((CONV_W, CONV_CH)),
                  pl.BlockSpec((1, SUBLANES, CONV_CH), lambda bi, t: (bi, 0, 0)),
                  pl.BlockSpec((1, H_LIN, HEAD_DIM, HEAD_DIM), lambda bi, t: (bi, 0, 0, 0)),
                  const((1, LANES)), const((1, LANES)), const((1, HEAD_DIM)), const((LIN_W, LIN_W))],
        out_specs=[tile(LIN_W),
                   pl.BlockSpec((1, H_LIN, HEAD_DIM, HEAD_DIM), lambda bi, t: (bi, 0, 0, 0)),
                   pl.BlockSpec((1, CONV_W - 1, CONV_CH), lambda bi, t: (bi, 0, 0))],
        out_shape=[jax.ShapeDtypeStruct((b, tp, LIN_W), F32),
                   jax.ShapeDtypeStruct((b, H_LIN, HEAD_DIM, HEAD_DIM), F32),
                   jax.ShapeDtypeStruct((b, CONV_W - 1, CONV_CH), F32)],
        scratch_shapes=[pltpu.VMEM((c + 2 * SUBLANES, CONV_CH), F32),
                        pltpu.VMEM((H_LIN, HEAD_DIM, HEAD_DIM), F32)],
        compiler_params=_cparams(2),
        name="gdn",
    )(lin, z, ba, conv_w, cbuf8, s0, alog_l, dtb_l, og, bd)


def _outproj_body(oa_ref, ol_ref, x_ref, ga_ref, sc_ref, sh_ref, gn_ref, wo_ref, wqt_ref,
                  x1_ref, h2_ref, qt_ref):
    y = (jnp.dot(oa_ref[...].astype(BF16), wo_ref[:ATTN_W, :], preferred_element_type=F32)
         + jnp.dot(ol_ref[...].astype(BF16), wo_ref[ATTN_W:, :], preferred_element_type=F32))
    x1 = x_ref[...] + ga_ref[0] * y
    ms = jnp.mean(x1 * x1, axis=-1, keepdims=True)
    h2 = x1 * lax.rsqrt(ms + EPS) * gn_ref[...]
    h2 = (h2 * (1.0 + sc_ref[0]) + sh_ref[0]).astype(BF16)
    x1_ref[...] = x1
    h2_ref[...] = h2
    qt_ref[...] = lax.dot_general(wqt_ref[...], h2, (((1,), (1,)), ((), ())),
                                  preferred_element_type=F32)


def _outproj(oa, ol, x, ga, sc, sh, gn, wo, wqt, tm, tiles_per_mod):
    n, d = x.shape
    r = sc.shape[1]
    pq = wqt.shape[0]
    mod_spec = pl.BlockSpec((1, r, d), lambda i: (i // tiles_per_mod, 0, 0))
    const = lambda shape: pl.BlockSpec(shape, lambda i: (0,) * len(shape))
    row = lambda w: pl.BlockSpec((tm, w), lambda i: (i, 0))
    return pl.pallas_call(
        _outproj_body,
        grid=(n // tm,),
        in_specs=[row(ATTN_W), row(LIN_W), row(d), mod_spec, mod_spec, mod_spec, const((1, d)),
                  const(wo.shape), const(wqt.shape)],
        out_specs=[row(d), row(d), pl.BlockSpec((pq, tm), lambda i: (0, i))],
        out_shape=[jax.ShapeDtypeStruct((n, d), F32), jax.ShapeDtypeStruct((n, d), BF16),
                   jax.ShapeDtypeStruct((pq, n), F32)],
        compiler_params=_cparams(1),
        name="outproj",
    )(oa, ol, x, ga, sc, sh, gn, wo, wqt)


def _topk_rows(s, k):
    rows, t = s.shape
    iota = lax.broadcasted_iota(I32, (rows, t), 0).astype(F32)
    kio = lax.broadcasted_iota(I32, (k, t), 0)

    def body(j, carry):
        cur, rank, vals, pos = carry
        m = jnp.max(cur, axis=0, keepdims=True)
        idx = jnp.min(jnp.where(cur == m, iota, float(rows)), axis=0, keepdims=True)
        hit = iota == idx
        rank = jnp.where(hit, j.astype(F32), rank)
        cur = jnp.where(hit, NEG_INF, cur)
        vals = jnp.where(kio == j, m, vals)
        pos = jnp.where(kio == j, idx, pos)
        return cur, rank, vals, pos

    init = (s, jnp.full((rows, t), float(k), F32), jnp.zeros((k, t), F32), jnp.zeros((k, t), F32))
    _, rank, vals, pos = lax.fori_loop(0, k, body, init)
    return vals, pos, rank


def _route_body(qt_ref, keys_ref, e1_ref, c1_ref, e2_ref, r2_ref):
    kk = PEER_TOPK
    half = keys_ref.shape[3]

    def per_head(h, _):
        s1 = _dot3(keys_ref[h, 0], qt_ref[pl.ds(pl.multiple_of(h * 2 * half, half), half), :])
        s2 = _dot3(keys_ref[h, 1], qt_ref[pl.ds(pl.multiple_of(h * 2 * half + half, half), half), :])
        v1, _, rank1 = _topk_rows(s1, kk)
        v2, _, rank2 = _topk_rows(s2, kk)
        cand = jnp.concatenate([v1[r:r + 1, :] + v2 for r in range(kk)], axis=0)
        cv, cpos, _ = _topk_rows(cand, kk)
        r1sel = jnp.floor(cpos * (1.0 / kk))
        cnt1 = jnp.zeros(s1.shape, F32)
        for j in range(kk):
            cnt1 = cnt1 + jnp.where(rank1 == r1sel[j:j + 1, :], 1.0, 0.0)
        z = jnp.sum(jnp.exp(cv - cv[0:1, :]), axis=0, keepdims=True)
        e1 = jnp.where(rank1 < kk, jnp.exp(s1 - v1[0:1, :]), 0.0) / z
        e2 = jnp.where(rank2 < kk, jnp.exp(s2 - v2[0:1, :]), 0.0)
        e1_ref[h] = e1
        c1_ref[h] = cnt1
        e2_ref[h] = e2
        r2_ref[h] = rank2
        return 0

    lax.fori_loop(0, PEER_HEADS, per_head, 0)


def _peer_route(qt, keys, tn):
    pq, n = qt.shape
    out = jax.ShapeDtypeStruct((PEER_HEADS, N_KEYS, n), F32)
    ospec = pl.BlockSpec((PEER_HEADS, N_KEYS, tn), lambda i: (0, 0, i))
    return pl.pallas_call(
        _route_body,
        grid=(n // tn,),
        in_specs=[pl.BlockSpec((pq, tn), lambda i: (0, i)),
                  pl.BlockSpec(keys.shape, lambda i: (0, 0, 0, 0))],
        out_specs=[ospec] * 4,
        out_shape=[out] * 4,
        compiler_params=_cparams(1),
        name="peer_route",
    )(qt, keys)


def _erf(x):
    return lax.erf(x)


def _peer_body(h2_ref, u_ref, vt_ref, e1_ref, c1_ref, e2_ref, r2_ref, x1_ref, gf_ref, o_ref,
               acc, ga, *, a_per_tile):
    e = pl.program_id(1)

    @pl.when(e == 0)
    def _zero():
        acc[...] = jnp.zeros(acc.shape, F32)

    for al in range(a_per_tile):
        rows = slice(al * N_KEYS, (al + 1) * N_KEYS)
        pre = lax.dot_general(u_ref[rows, :], h2_ref[...], (((1,), (1,)), ((), ())),
                              preferred_element_type=F32)
        act = 0.5 * pre * (1.0 + _erf(pre * (2.0 ** -0.5)))
        g = jnp.zeros(pre.shape, F32)
        for h in range(PEER_HEADS):
            hit = r2_ref[h] < c1_ref[h, al:al + 1, :]
            g = g + jnp.where(hit, e2_ref[h], 0.0) * e1_ref[h, al:al + 1, :]
        ga[rows, :] = (g * act).astype(BF16)
    acc[...] += jnp.dot(vt_ref[...], ga[...], preferred_element_type=F32)

    @pl.when(e == pl.num_programs(1) - 1)
    def _done():
        o_ref[...] = x1_ref[...] + gf_ref[0] * acc[...].T


def _peer_dense(h2, u_bf, vt_bf, e1, c1, e2, r2, x1, gf, tn, te, tiles_per_mod):
    n, d = x1.shape
    n_e = u_bf.shape[0]
    a_per_tile = te // N_KEYS
    r = gf.shape[1]
    key_rows = lambda: pl.BlockSpec((PEER_HEADS, a_per_tile, tn), lambda i, e: (0, e, i))
    key_full = lambda: pl.BlockSpec((PEER_HEADS, N_KEYS, tn), lambda i, e: (0, 0, i))
    return pl.pallas_call(
        functools.partial(_peer_body, a_per_tile=a_per_tile),
        grid=(n // tn, n_e // te),
        in_specs=[pl.BlockSpec((tn, d), lambda i, e: (i, 0)),
                  pl.BlockSpec((te, d), lambda i, e: (e, 0)),
                  pl.BlockSpec((d, te), lambda i, e: (0, e)),
                  key_rows(), key_rows(), key_full(), key_full(),
                  pl.BlockSpec((tn, d), lambda i, e: (i, 0)),
                  pl.BlockSpec((1, r, d), lambda i, e: (i // tiles_per_mod, 0, 0))],
        out_specs=pl.BlockSpec((tn, d), lambda i, e: (i, 0)),
        out_shape=jax.ShapeDtypeStruct((n, d), F32),
        scratch_shapes=[pltpu.VMEM((d, tn), F32), pltpu.VMEM((te, tn), BF16)],
        compiler_params=_cparams(2),
        name="peer_dense",
    )(h2, u_bf, vt_bf, e1, c1, e2, r2, x1, gf)


def _layer_group(x, mods, p, attend, conv_buf, s0, tm, tn_route, tn_peer):
    b, t, d = x.shape
    n = b * t
    sh_a, sc_a, g_a, sh_f, sc_f, g_f = mods
    tiles_per_mod = max(t // tm, 1) if sh_a.shape[0] > 1 else n // tm
    xf = x.reshape(n, d)
    q, k, v, lin, z, ba, km = _inproj(xf, sc_a, sh_a, p["norm_attn"], p["w_in"], p["qg"], p["kg"],
                                      p["bd"], tm, tiles_per_mod)
    q = q.reshape(b, t, ATTN_W)
    k = k.reshape(b, t, KV_W)
    v = v.reshape(b, t, KV_W)
    o_attn = attend(q, k, v, km)
    tp = -(-t // GDN_CHUNK) * GDN_CHUNK
    pad3 = lambda a: jnp.pad(a.reshape(b, t, -1), ((0, 0), (0, tp - t), (0, 0)))
    cbuf8 = jnp.pad(conv_buf, ((0, 0), (SUBLANES - (CONV_W - 1), 0), (0, 0)))
    o_lin, s_new, conv_new = _gdn(pad3(lin), pad3(z), pad3(ba), p["conv_w"], cbuf8, s0,
                                  p["alog"], p["dtb"], p["og"], p["bd"], t)
    o_lin = o_lin[:, :t].reshape(n, LIN_W)
    x1, h2, qt = _outproj(o_attn.reshape(n, ATTN_W), o_lin, xf, g_a, sc_f, sh_f, p["norm_ffn"],
                          p["w_out"], p["wqt"], tm, tiles_per_mod)
    e1, c1, e2, r2 = _peer_route(qt, p["peer_keys"], tn_route)
    te = 1024
    x2 = _peer_dense(h2, p["u_bf"], p["vt_bf"], e1, c1, e2, r2, x1, g_f, tn_peer, te,
                     max(t // tn_peer, 1) if sh_a.shape[0] > 1 else n // tn_peer)
    return (x2.reshape(b, t, d), k.reshape(b, t, N_KV_HEADS, HEAD_DIM),
            v.reshape(b, t, N_KV_HEADS, HEAD_DIM), s_new, conv_new)


def kernel(x_prompt, x_sample, cache_k, cache_v, state_ssm, state_conv, page_table, c_prompt, c_sample,
           rel_bias, w_ada, b_ada, norm_attn, norm_ffn, w_in, w_out, q_norm, k_norm, conv_w, a_log,
           dt_bias, o_norm, peer_wq, peer_keys, peer_u, peer_v):
    depth = w_ada.shape[0]
    b, t, d = x_prompt.shape
    b2, s, _ = x_sample.shape
    n_pool = cache_k.shape[1]
    n_pages = page_table.shape[1]
    past = n_pages * PAGE_SIZE
    assert t % MOBA_BLOCK == 0 and past % MOBA_BLOCK == 0 and n_pages % PAGES_PER_STEP == 0
    assert MOBA_BLOCK + 1 >= MAX_DISTANCE and s <= 16 and w_in.shape[2] + LANES - 2 * H_LIN == D_IN_PAD
    n_blocks = past // MOBA_BLOCK

    mod = _ada(jnp.concatenate([c_prompt, c_sample], axis=0), w_ada, b_ada)
    btab = _relbias(rel_bias)
    rb_flat = rel_bias.reshape(-1)

    ck_flat = cache_k.reshape(depth * n_pool, PAGE_SIZE, KV_W)
    cv_flat = cache_v.reshape(depth * n_pool, PAGE_SIZE, KV_W)
    bd = (jnp.arange(ATTN_W)[:, None] // HEAD_DIM == jnp.arange(ATTN_W)[None, :] // HEAD_DIM).astype(BF16)

    rows = N_HEADS * s
    s_pad = 16
    bias_last = btab[1, :, :s, :].reshape(rows, MOBA_BLOCK)
    own = btab[0, :, :s, :s_pad]
    causal = jnp.arange(s_pad)[None, :] <= jnp.arange(s)[:, None]
    bias_own = jnp.where(causal[None], own, NEG_INF).reshape(rows, s_pad)
    far = jnp.repeat(rel_bias[N_BUCKETS - 1], s).reshape(rows, 1)
    kv_of_row = jnp.arange(rows) // (s * GQA)
    lane_kv = jnp.arange(KV_W) // HEAD_DIM
    row_mask = (kv_of_row[:, None] == lane_kv[None, :]).astype(F32)

    xp, xs = x_prompt, x_sample
    outs = [[] for _ in range(8)]
    for l in range(depth):
        lane_pad = lambda a: jnp.pad(a[l][None, :], ((0, 0), (H_LIN, LANES - 2 * H_LIN)))
        p = {
            "norm_attn": norm_attn[l][None, :], "norm_ffn": norm_ffn[l][None, :],
            "w_in": jnp.pad(w_in[l], ((0, 0), (0, D_IN_PAD - w_in.shape[2]))).astype(BF16),
            "w_out": w_out[l].astype(BF16),
            "qg": jnp.tile(q_norm[l], N_HEADS)[None, :], "kg": jnp.tile(k_norm[l], N_KV_HEADS)[None, :],
            "bd": bd, "conv_w": conv_w[l], "alog": lane_pad(a_log), "dtb": lane_pad(dt_bias),
            "og": o_norm[l][None, :], "wqt": peer_wq[l].T.astype(BF16), "peer_keys": peer_keys[l],
            "u_bf": peer_u[l].astype(BF16), "vt_bf": peer_v[l].T.astype(BF16),
        }
        m6 = jnp.split(mod[l], 6, axis=-1)
        mods_p = [m[:b][:, None, :] for m in m6]
        mods_s = [jnp.repeat(m[b:], s, axis=0)[None] for m in m6]

        def attend_p(q, k, v, km):
            return _moba_prompt(rb_flat, q, k, v, km.reshape(b, t // MOBA_BLOCK, KV_W), btab)

        pt_flat = (page_table + l * n_pool).reshape(-1).astype(I32)

        def attend_s(q, k, v, km):
            kmean = _kmean_pages(pt_flat, ck_flat, b2, n_pages)
            qr = q.reshape(b2, s, N_HEADS, HEAD_DIM).transpose(0, 2, 1, 3).reshape(b2, rows, HEAD_DIM)
            qw = jnp.tile(qr, (1, 1, N_KV_HEADS)) * row_mask[None]
            padn = lambda a: jnp.pad(a, ((0, 0), (0, s_pad - s), (0, 0)))
            o = _moba_sample(pt_flat, ck_flat, cv_flat, qw, kmean, padn(k), padn(v),
                             bias_last, bias_own, far, n_pages)
            return o.reshape(b2, N_HEADS, s, HEAD_DIM).transpose(0, 2, 1, 3).reshape(b2, s, ATTN_W)

        conv0 = jnp.zeros((b, CONV_W - 1, CONV_CH), F32)
        ssm0 = jnp.zeros((b, H_LIN, HEAD_DIM, HEAD_DIM), F32)
        xp, kp, vp, sp, cp = _layer_group(xp, mods_p, p, attend_p, conv0, ssm0, 256, 256, 256)
        xs, ks, vs, ss, cs = _layer_group(xs, mods_s, p, attend_s, state_conv[l], state_ssm[l],
                                          b2 * s, b2 * s, b2 * s)
        for lst, val in zip(outs, (kp, vp, ks, vs, sp, ss, cp, cs)):
            lst.append(val)
    return (xp, xs) + tuple(jnp.stack(o) for o in outs)
```

```python
import functools
import math

import numpy as np
import jax
import jax.numpy as jnp
from jax import lax
from jax.experimental import pallas as pl
from jax.experimental.pallas import tpu as pltpu

F32 = jnp.float32
BF16 = jnp.bfloat16
I32 = jnp.int32

HEAD_DIM = 64
N_HEADS = 8
N_KV_HEADS = 4
GQA = N_HEADS // N_KV_HEADS
ATTN_W = N_HEADS * HEAD_DIM
KV_W = N_KV_HEADS * HEAD_DIM
MOBA_BLOCK = 256
MOBA_TOPK = 3
N_BUCKETS = 32
MAX_DISTANCE = 128
H_LIN = 8
LIN_W = H_LIN * HEAD_DIM
CONV_W = 4
CONV_CH = 3 * LIN_W
GDN_CHUNK = 64
GDN_CHUNKS_PER_STEP = 4
INV_BLOCK = 16
PEER_HEADS = 8
N_KEYS = 128
PEER_TOPK = 16
PAGE_SIZE = 128
EPS = 1e-6

LANES = 128
SUBLANES = 8
VMEM_LIMIT_BYTES = 56 * 1024 * 1024

D_IN_PAD = ATTN_W + 2 * KV_W + CONV_CH + LIN_W + LANES
NEG_INF = float("-inf")

_NT = (((1,), (1,)), ((), ()))
_TN = (((0,), (0,)), ((), ()))


def _cparams(n_axes):
    return pltpu.CompilerParams(dimension_semantics=("arbitrary",) * n_axes,
                                vmem_limit_bytes=VMEM_LIMIT_BYTES)


def _round_up(x, m):
    return -(-x // m) * m


def _dot(a, b):
    return jnp.dot(a.astype(BF16), b.astype(BF16), preferred_element_type=F32)


def _dot_nt(a, b):
    return lax.dot_general(a.astype(BF16), b.astype(BF16), _NT, preferred_element_type=F32)


def _dot_tn(a, b):
    return lax.dot_general(a.astype(BF16), b.astype(BF16), _TN, preferred_element_type=F32)


def _bmm(a, b):
    return jnp.einsum("bij,bjk->bik", a.astype(BF16), b.astype(BF16), preferred_element_type=F32)


def _bmm_nt(a, b):
    return jnp.einsum("bik,bjk->bij", a.astype(BF16), b.astype(BF16), preferred_element_type=F32)


def _split2(a):
    hi = a.astype(BF16)
    lo = (a - hi.astype(F32)).astype(BF16)
    return hi, lo


def _split3(a):
    hi = a.astype(BF16)
    r = a - hi.astype(F32)
    mid = r.astype(BF16)
    lo = (r - mid.astype(F32)).astype(BF16)
    return hi, mid, lo


def _dot3(a, b):
    ah, al = _split2(a)
    bh, bl = _split2(b)
    d = functools.partial(jnp.dot, preferred_element_type=F32)
    return d(ah, bh) + (d(al, bh) + d(ah, bl))


def _dot3_nt(a, b):
    ah, al = _split2(a)
    bh, bl = _split2(b)
    d = functools.partial(lax.dot_general, dimension_numbers=_NT, preferred_element_type=F32)
    return d(ah, bh) + (d(al, bh) + d(ah, bl))


def _dot_mask_rhs(a, mask_bf16):
    hi, mid, lo = _split3(a)
    d = functools.partial(jnp.dot, preferred_element_type=F32)
    return d(hi, mask_bf16) + (d(mid, mask_bf16) + d(lo, mask_bf16))


def _dot_mask_lhs(mask_bf16, b):
    hi, mid, lo = _split3(b)
    d = functools.partial(jnp.dot, preferred_element_type=F32)
    return d(mask_bf16, hi) + (d(mask_bf16, mid) + d(mask_bf16, lo))


def _sigmoid(x):
    return 1.0 / (1.0 + jnp.exp(-x))


def _softplus(x):
    return jnp.maximum(x, 0.0) + jnp.log(1.0 + jnp.exp(-jnp.abs(x)))


def _head_sumsq(a, bd):
    return _dot_mask_rhs(a * a, bd)


def _rank_desc(g, idx, n, axis):
    rank = jnp.zeros(g.shape, F32)
    for m in range(n):
        gm = lax.slice_in_dim(g, m, m + 1, axis=axis)
        beats = (gm > g) | ((gm == g) & (idx > m))
        rank = rank + jnp.where(beats, 1.0, 0.0)
    return rank


def _ada_body(c_ref, w_ref, b_ref, o_ref):
    c = c_ref[...]
    s = c * _sigmoid(c)
    o_ref[0] = _dot3(s, w_ref[0]) + b_ref[0]


def _ada(c, w_ada, b_ada):
    n_l, d, d6 = w_ada.shape
    r = c.shape[0]
    tn = 512
    return pl.pallas_call(
        _ada_body,
        grid=(n_l, d6 // tn),
        in_specs=[pl.BlockSpec((r, d), lambda l, j: (0, 0)),
                  pl.BlockSpec((1, d, tn), lambda l, j: (l, 0, j)),
                  pl.BlockSpec((1, 1, tn), lambda l, j: (l, 0, j))],
        out_specs=pl.BlockSpec((1, r, tn), lambda l, j: (l, 0, j)),
        out_shape=jax.ShapeDtypeStruct((n_l, r, d6), F32),
        compiler_params=_cparams(2),
        name="ada",
    )(c, w_ada, b_ada.reshape(n_l, 1, d6))


def _inproj_body(x_ref, sc_ref, sh_ref, gn_ref, w_ref, qg_ref, kg_ref, bd_ref,
                 q_ref, k_ref, v_ref, kt_ref, vt_ref, lin_ref, z_ref, ba_ref, km_ref):
    x = x_ref[...]
    ms = jnp.mean(x * x, axis=-1, keepdims=True)
    h = x * lax.rsqrt(ms + EPS) * gn_ref[...]
    h = h * (1.0 + sc_ref[0]) + sh_ref[0]
    proj = jnp.dot(h.astype(BF16), w_ref[...], preferred_element_type=F32)
    bd = bd_ref[...]
    aq = proj[:, :ATTN_W]
    ak = proj[:, ATTN_W:ATTN_W + KV_W]
    q = aq * lax.rsqrt(_head_sumsq(aq, bd) * (1.0 / HEAD_DIM) + EPS) * qg_ref[...]
    k = ak * lax.rsqrt(_head_sumsq(ak, bd[:KV_W, :KV_W]) * (1.0 / HEAD_DIM) + EPS) * kg_ref[...]
    q_ref[...] = q * (HEAD_DIM ** -0.5)
    k_ref[...] = k
    kt_ref[0] = k.T
    o = ATTN_W + KV_W
    v = proj[:, o:o + KV_W]
    v_ref[...] = v
    vt_ref[0] = v.T
    o += KV_W
    lin_ref[...] = proj[:, o:o + CONV_CH]
    o += CONV_CH
    z_ref[...] = proj[:, o:o + LIN_W]
    o += LIN_W
    ba_ref[...] = proj[:, o:o + LANES]
    km_ref[0] = jnp.mean(k, axis=0, keepdims=True)


def _inproj(x, sc, sh, gn, w_pad, qg, kg, bd, tm, tiles_per_mod, tiles_per_seq):
    n, d = x.shape
    r = sc.shape[1]
    nt = n // tm
    n_seq = nt // tiles_per_seq
    mod_spec = pl.BlockSpec((1, r, d), lambda i: (i // tiles_per_mod, 0, 0))
    const = lambda shape: pl.BlockSpec(shape, lambda i: (0,) * len(shape))
    row = lambda w: pl.BlockSpec((tm, w), lambda i: (i, 0))
    tspec = pl.BlockSpec((1, KV_W, tm), lambda i: (i // tiles_per_seq, 0, i % tiles_per_seq))
    tshape = jax.ShapeDtypeStruct((n_seq, KV_W, tiles_per_seq * tm), F32)
    return pl.pallas_call(
        _inproj_body,
        grid=(nt,),
        in_specs=[row(d), mod_spec, mod_spec, const((1, d)), const(w_pad.shape),
                  const((1, ATTN_W)), const((1, KV_W)), const((ATTN_W, ATTN_W))],
        out_specs=[row(ATTN_W), row(KV_W), row(KV_W), tspec, tspec, row(CONV_CH), row(LIN_W), row(LANES),
                   pl.BlockSpec((1, 1, KV_W), lambda i: (i, 0, 0))],
        out_shape=[jax.ShapeDtypeStruct((n, ATTN_W), F32), jax.ShapeDtypeStruct((n, KV_W), F32),
                   jax.ShapeDtypeStruct((n, KV_W), F32), tshape, tshape,
                   jax.ShapeDtypeStruct((n, CONV_CH), F32),
                   jax.ShapeDtypeStruct((n, LIN_W), F32), jax.ShapeDtypeStruct((n, LANES), F32),
                   jax.ShapeDtypeStruct((nt, 1, KV_W), F32)],
        compiler_params=_cparams(1),
        name="inproj",
    )(x, sc, sh, gn, w_pad, qg, kg, bd)


def _relbias_body(rb_ref, o_ref):
    n_tab, n_h, r, c = o_ref.shape
    key = lax.broadcasted_iota(I32, (r, c), 0)
    qry = lax.broadcasted_iota(I32, (r, c), 1)
    max_exact = N_BUCKETS // 2
    for t in range(n_tab):
        dist = jnp.maximum(qry - key + t * MOBA_BLOCK, 0)
        nf = jnp.maximum(dist, 1).astype(F32)
        large = max_exact + (jnp.log(nf / max_exact) / math.log(MAX_DISTANCE / max_exact)
                             * (N_BUCKETS - max_exact)).astype(I32)
        bucket = jnp.where(dist < max_exact, dist, jnp.minimum(large, N_BUCKETS - 1))

        def per_head(h, _, bucket=bucket, t=t):
            acc = jnp.zeros((r, c), F32)
            for j in range(N_BUCKETS):
                acc = jnp.where(bucket == j, rb_ref[j * n_h + h], acc)
            o_ref[t, h] = acc
            return 0

        lax.fori_loop(0, n_h, per_head, 0)


def _relbias(rel_bias):
    n_h = rel_bias.shape[1]
    return pl.pallas_call(
        _relbias_body,
        in_specs=[pl.BlockSpec(memory_space=pltpu.SMEM)],
        out_specs=pl.BlockSpec(memory_space=pltpu.VMEM),
        out_shape=jax.ShapeDtypeStruct((2, n_h, MOBA_BLOCK, MOBA_BLOCK), F32),
        compiler_params=pltpu.CompilerParams(vmem_limit_bytes=VMEM_LIMIT_BYTES),
        name="relbias",
    )(rel_bias.reshape(-1))


def _moba_prompt_body(rb_ref, q_ref, km_ref, k_ref, vt_ref, bown_ref, bprev_ref, o_ref,
                      selt, m_s, l_s, acct):
    i = pl.program_id(1)
    n = pl.program_id(2)
    nbp = km_ref.shape[1]
    blk = q_ref.shape[1]
    n_steps = pl.num_programs(2)
    far_bucket = N_BUCKETS - 1

    def hs(h):
        return slice(h * HEAD_DIM, (h + 1) * HEAD_DIM)

    def scores(h):
        kv = h // GQA
        return _dot_nt(k_ref[0, :, hs(kv)], q_ref[0, :, hs(h)])

    def vth(h):
        kv = h // GQA
        return vt_ref[0, kv * HEAD_DIM:(kv + 1) * HEAD_DIM, :]

    @pl.when(n == 0)
    def _first():
        row = lax.broadcasted_iota(I32, (nbp, blk), 0)
        kr = lax.broadcasted_iota(I32, (blk, blk), 0)
        qc = lax.broadcasted_iota(I32, (blk, blk), 1)
        causal = kr <= qc
        for h in range(N_HEADS):
            kv = h // GQA
            gate = _dot3_nt(km_ref[0, :, hs(kv)], q_ref[0, :, hs(h)])
            g = jnp.where(row < i, gate, NEG_INF)
            rank = _rank_desc(g, row, nbp, 0)
            selt[h] = jnp.where((rank < MOBA_TOPK) & (row < i), 1.0, 0.0)
            s = jnp.where(causal, scores(h) + bown_ref[h], NEG_INF)
            m = jnp.max(s, axis=0, keepdims=True)
            p = jnp.exp(s - m)
            m_s[h] = m
            l_s[h] = jnp.sum(p, axis=0, keepdims=True)
            acct[hs(h), :] = _dot(vth(h), p)

    @pl.when((n >= 1) & (n <= i))
    def _past():
        kb = n - 1
        is_prev = kb == i - 1
        for h in range(N_HEADS):
            bias = jnp.where(is_prev, bprev_ref[h], rb_ref[far_bucket * N_HEADS + h])
            sel = selt[h, pl.ds(kb, 1), :]
            s = jnp.where(sel > 0.5, scores(h) + bias, NEG_INF)
            m_old = m_s[h]
            m_new = jnp.maximum(m_old, jnp.max(s, axis=0, keepdims=True))
            alpha = jnp.exp(m_old - m_new)
            p = jnp.exp(s - m_new)
            m_s[h] = m_new
            l_s[h] = alpha * l_s[h] + jnp.sum(p, axis=0, keepdims=True)
            acct[hs(h), :] = alpha * acct[hs(h), :] + _dot(vth(h), p)

    @pl.when(n == n_steps - 1)
    def _done():
        for h in range(N_HEADS):
            acct[hs(h), :] = acct[hs(h), :] / l_s[h]
        o_ref[0] = acct[...].T


def _moba_prompt(rb_flat, q, k, vt, kmean, btab):
    b, t, _ = q.shape
    nb = t // MOBA_BLOCK
    nbp = kmean.shape[1]
    blk = MOBA_BLOCK

    def kv_blk(i, n):
        past = jnp.clip(n - 1, 0, jnp.maximum(i - 1, 0))
        return jnp.where(n == 0, i, past)

    return pl.pallas_call(
        _moba_prompt_body,
        grid=(b, nb, nb),
        in_specs=[pl.BlockSpec(memory_space=pltpu.SMEM),
                  pl.BlockSpec((1, blk, ATTN_W), lambda bi, i, n: (bi, i, 0)),
                  pl.BlockSpec((1, nbp, KV_W), lambda bi, i, n: (bi, 0, 0)),
                  pl.BlockSpec((1, blk, KV_W), lambda bi, i, n: (bi, kv_blk(i, n), 0)),
                  pl.BlockSpec((1, KV_W, blk), lambda bi, i, n: (bi, 0, kv_blk(i, n))),
                  pl.BlockSpec((None, N_HEADS, blk, blk), lambda bi, i, n: (0, 0, 0, 0)),
                  pl.BlockSpec((None, N_HEADS, blk, blk), lambda bi, i, n: (1, 0, 0, 0))],
        out_specs=pl.BlockSpec((1, blk, ATTN_W), lambda bi, i, n: (bi, i, 0)),
        out_shape=jax.ShapeDtypeStruct((b, t, ATTN_W), F32),
        scratch_shapes=[pltpu.VMEM((N_HEADS, nbp, blk), F32),
                        pltpu.VMEM((N_HEADS, 1, blk), F32),
                        pltpu.VMEM((N_HEADS, 1, blk), F32),
                        pltpu.VMEM((ATTN_W, blk), F32)],
        compiler_params=_cparams(3),
        name="moba_prompt",
    )(rb_flat, q, kmean, k, vt, btab, btab)


PAGES_PER_STEP = 16
PAGES_PER_BLOCK = MOBA_BLOCK // PAGE_SIZE
BLOCKS_PER_STEP = PAGES_PER_STEP // PAGES_PER_BLOCK


def _moba_sample_body(pt_ref, qw_ref, kn_ref, vn_ref, bl_ref, bo_ref, far_ref, *refs, n_blocks):
    kp = refs[:PAGES_PER_STEP]
    vp = refs[PAGES_PER_STEP:2 * PAGES_PER_STEP]
    o_ref, kres, kmt, selc, m_s, l_s, acc = refs[2 * PAGES_PER_STEP:]
    ph = pl.program_id(1)
    j = pl.program_id(2)
    n_steps = pl.num_programs(2)
    rows = qw_ref.shape[1]
    qw = qw_ref[0]
    qw_b = qw.astype(BF16)

    @pl.when(ph == 0)
    def _scan_keys():
        lane = lax.broadcasted_iota(I32, (KV_W, LANES), 1)

        @pl.when(j == 0)
        def _zero():
            kmt[...] = jnp.zeros(kmt.shape, F32)

        cur = kmt[...]
        for nn in range(BLOCKS_PER_STEP):
            tot = kp[nn * PAGES_PER_BLOCK][0]
            for p in range(1, PAGES_PER_BLOCK):
                tot = tot + kp[nn * PAGES_PER_BLOCK + p][0]
            col = jnp.sum(tot, axis=1, keepdims=True) * (1.0 / MOBA_BLOCK)
            cur = jnp.where(lane == j * BLOCKS_PER_STEP + nn, col, cur)
            for p in range(PAGES_PER_BLOCK):
                r = nn * PAGES_PER_BLOCK + p
                kres[j * PAGES_PER_STEP + r] = kp[r][0].astype(BF16)
        kmt[...] = cur

    @pl.when((ph == 1) & (j == 0))
    def _select():
        gate = _dot3(qw, kmt[...])
        col = lax.broadcasted_iota(I32, (rows, LANES), 1)
        g = jnp.where(col < n_blocks, gate, NEG_INF)
        rank = _rank_desc(g, col, n_blocks, 1)
        sel = jnp.where((rank < MOBA_TOPK) & (col < n_blocks), 1.0, 0.0)
        for jj in range(n_blocks // BLOCKS_PER_STEP):
            selc[jj] = sel[:, jj * BLOCKS_PER_STEP:(jj + 1) * BLOCKS_PER_STEP]
        s = _dot_nt(qw_b, kn_ref[0]) + bo_ref[...]
        m = jnp.max(s, axis=-1, keepdims=True)
        p = jnp.exp(s - m)
        m_s[...] = m
        l_s[...] = jnp.sum(p, axis=-1, keepdims=True)
        acc[...] = _dot(p, vn_ref[0])

    @pl.when(ph == 1)
    def _attend():
        sel_j = selc[j]
        pieces = []
        for nn in range(BLOCKS_PER_STEP):
            s = jnp.concatenate(
                [jnp.dot(qw_b, kres[j * PAGES_PER_STEP + nn * PAGES_PER_BLOCK + p],
                         preferred_element_type=F32) for p in range(PAGES_PER_BLOCK)], axis=1)
            if nn == BLOCKS_PER_STEP - 1:
                s = s + jnp.where(j == n_steps - 1, bl_ref[...], far_ref[...])
            else:
                s = s + far_ref[...]
            pieces.append(jnp.where(sel_j[:, nn:nn + 1] > 0.5, s, NEG_INF))
        s_all = jnp.concatenate(pieces, axis=1)
        m_old = m_s[...]
        m_new = jnp.maximum(m_old, jnp.max(s_all, axis=-1, keepdims=True))
        alpha = jnp.exp(m_old - m_new)
        p_all = jnp.exp(s_all - m_new)
        m_s[...] = m_new
        l_s[...] = alpha * l_s[...] + jnp.sum(p_all, axis=-1, keepdims=True)
        p_b = p_all.astype(BF16)
        pv = jnp.zeros(acc.shape, F32)
        for r in range(PAGES_PER_STEP):
            pv = pv + lax.dot_general(p_b[:, r * PAGE_SIZE:(r + 1) * PAGE_SIZE], vp[r][0].astype(BF16),
                                      _NT, preferred_element_type=F32)
        acc[...] = alpha * acc[...] + pv

    @pl.when((ph == 1) & (j == n_steps - 1))
    def _done():
        out = acc[...] / l_s[...]
        rpk = rows // N_KV_HEADS
        for kv in range(N_KV_HEADS):
            o_ref[0, kv * rpk:(kv + 1) * rpk, :] = out[kv * rpk:(kv + 1) * rpk,
                                                      kv * HEAD_DIM:(kv + 1) * HEAD_DIM]


def _moba_sample(pt_flat, ckt, cvt, qw, kn, vn, bias_last, bias_own, far, n_pages):
    b2, rows, _ = qw.shape
    n_blocks = n_pages // PAGES_PER_BLOCK
    n_steps = n_pages // PAGES_PER_STEP
    s_pad = kn.shape[1]
    last = n_steps - 1

    def k_spec(r):
        return pl.BlockSpec((1, KV_W, PAGE_SIZE), lambda b, ph, j, pt: (
            pt[b * n_pages + jnp.where(ph == 0, j, last) * PAGES_PER_STEP + r], 0, 0))

    def v_spec(r):
        return pl.BlockSpec((1, KV_W, PAGE_SIZE), lambda b, ph, j, pt: (
            pt[b * n_pages + jnp.where(ph == 0, 0, j) * PAGES_PER_STEP + r], 0, 0))

    per_b = lambda shape: pl.BlockSpec((1,) + shape, lambda b, ph, j, pt: (b, 0, 0))
    const = lambda shape: pl.BlockSpec(shape, lambda b, ph, j, pt: (0, 0))
    return pl.pallas_call(
        functools.partial(_moba_sample_body, n_blocks=n_blocks),
        grid_spec=pltpu.PrefetchScalarGridSpec(
            num_scalar_prefetch=1,
            grid=(b2, 2, n_steps),
            in_specs=[per_b((rows, KV_W)), per_b((s_pad, KV_W)), per_b((s_pad, KV_W)),
                      const((rows, MOBA_BLOCK)), const((rows, s_pad)), const((rows, 1))]
            + [k_spec(r) for r in range(PAGES_PER_STEP)] + [v_spec(r) for r in range(PAGES_PER_STEP)],
            out_specs=pl.BlockSpec((1, rows, HEAD_DIM), lambda b, ph, j, pt: (b, 0, 0)),
            scratch_shapes=[pltpu.VMEM((n_pages, KV_W, PAGE_SIZE), BF16),
                            pltpu.VMEM((KV_W, LANES), F32),
                            pltpu.VMEM((n_steps, rows, BLOCKS_PER_STEP), F32),
                            pltpu.VMEM((rows, 1), F32),
                            pltpu.VMEM((rows, 1), F32),
                            pltpu.VMEM((rows, KV_W), F32)]),
        out_shape=jax.ShapeDtypeStruct((b2, rows, HEAD_DIM), F32),
        compiler_params=_cparams(3),
        name="moba_sample",
    )(pt_flat, qw, kn, vn, bias_last, bias_own, far,
      *([ckt] * PAGES_PER_STEP), *([cvt] * PAGES_PER_STEP))


def _gdn_body(lin_ref, z_ref, ba_ref, cw_ref, cb_ref, s0_ref, al_ref, dtb_ref, og_ref, bd_ref,
              o_ref, sout_ref, cout_ref, xbuf, state, *, t_valid, nc):
    t = pl.program_id(1)
    n_t = pl.num_programs(1)
    c = GDN_CHUNK
    tt = nc * c
    halo = SUBLANES

    @pl.when(t == 0)
    def _init():
        xbuf[0:halo, :] = cb_ref[0]
        state[...] = s0_ref[0]

    @pl.when(t > 0)
    def _carry():
        xbuf[0:halo, :] = xbuf[tt:tt + halo, :]

    xbuf[halo:halo + tt, :] = lin_ref[0]
    first = halo - (CONV_W - 1)
    conv = cw_ref[0:1, :] * xbuf[pl.ds(first, tt), :]
    for i in range(1, CONV_W):
        conv = conv + cw_ref[i:i + 1, :] * xbuf[pl.ds(first + i, tt), :]
    a = conv * _sigmoid(conv)
    lq = a[:, :LIN_W]
    lk = a[:, LIN_W:2 * LIN_W]
    lv = a[:, 2 * LIN_W:]
    bd = bd_ref[...]
    lq = lq * lax.rsqrt(_head_sumsq(lq, bd) + EPS) * (HEAD_DIM ** -0.5)
    lk = lk * lax.rsqrt(_head_sumsq(lk, bd) + EPS)
    ba = ba_ref[0]
    beta_all = _sigmoid(ba)
    g_all = -jnp.exp(al_ref[...]) * _softplus(ba + dtb_ref[...])
    if t_valid % tt != 0:
        rid = t * tt + lax.broadcasted_iota(I32, (tt, 1), 0)
        ok = rid < t_valid
        lq = jnp.where(ok, lq, 0.0)
        lk = jnp.where(ok, lk, 0.0)
        lv = jnp.where(ok, lv, 0.0)
        beta_all = jnp.where(ok, beta_all, 0.0)
        g_all = jnp.where(ok, g_all, 0.0)

    it = lax.broadcasted_iota(I32, (tt, tt), 0)
    jt = lax.broadcasted_iota(I32, (tt, tt), 1)
    tril = jnp.where((it >= jt) & (it // c == jt // c), 1.0, 0.0).astype(BF16)
    gc_all = _dot_mask_lhs(tril, g_all)

    def heads(x):
        return jnp.stack([x[ci * c:(ci + 1) * c, h * HEAD_DIM:(h + 1) * HEAD_DIM]
                          for ci in range(nc) for h in range(H_LIN)], axis=0)

    def cols(x, off):
        return jnp.stack([x[ci * c:(ci + 1) * c, off + h:off + h + 1]
                          for ci in range(nc) for h in range(H_LIN)], axis=0)

    q, k, v = heads(lq), heads(lk), heads(lv)
    beta = cols(beta_all, 0)
    gc = cols(gc_all, H_LIN)
    ii = lax.broadcasted_iota(I32, (1, c, c), 1)
    jj = lax.broadcasted_iota(I32, (1, c, c), 2)
    lower = ii >= jj
    strict = ii > jj
    diag_blk = (ii // INV_BLOCK) == (jj // INV_BLOCK)
    gc_row = jnp.sum(jnp.where(ii == jj, gc, 0.0), axis=1, keepdims=True)
    eg = jnp.exp(gc)
    dmat = jnp.where(lower, jnp.exp(jnp.minimum(gc - gc_row, 0.0)), 0.0)
    kb = k * beta
    vb = v * beta
    lm = _bmm_nt(kb, k) * jnp.where(strict, dmat, 0.0)
    dm = jnp.where(diag_blk, lm, 0.0)
    nm = lm - dm
    xp = -dm
    p = _bmm(dm, dm)
    steps = int(math.log2(INV_BLOCK)) - 1
    for s in range(steps):
        xp = xp + p + _bmm(xp, p)
        if s + 1 < steps:
            p = _bmm(p, p)
    mm = nm + _bmm(xp, nm)
    yp = -mm
    pm = _bmm(mm, mm)
    msteps = int(math.log2(c // INV_BLOCK)) - 1
    for s in range(msteps):
        yp = yp + pm + _bmm(yp, pm)
        if s + 1 < msteps:
            pm = _bmm(pm, pm)
    tp = yp + xp + _bmm(yp, xp)
    kbg = kb * eg
    u = vb + _bmm(tp, vb)
    w = kbg + _bmm(tp, kbg)
    qk = _bmm_nt(q, k) * dmat
    qd = q * eg
    gl = gc[:, c - 1:c, :]
    kt = k * jnp.exp(gl - gc)
    egl = jnp.exp(gl)

    s_cur = state[...]
    for ci in range(nc):
        sl = slice(ci * H_LIN, (ci + 1) * H_LIN)
        v_new = u[sl] - _bmm(w[sl], s_cur)
        o = _bmm(qd[sl], s_cur) + _bmm(qk[sl], v_new)
        upd = jnp.stack([_dot_tn(kt[ci * H_LIN + h], v_new[h]) for h in range(H_LIN)], axis=0)
        s_cur = s_cur * egl[sl] + upd
        on = o * lax.rsqrt(jnp.mean(o * o, axis=-1, keepdims=True) + EPS) * og_ref[...]
        for h in range(H_LIN):
            zh = z_ref[0, ci * c:(ci + 1) * c, h * HEAD_DIM:(h + 1) * HEAD_DIM]
            o_ref[0, ci * c:(ci + 1) * c, h * HEAD_DIM:(h + 1) * HEAD_DIM] = on[h] * (zh * _sigmoid(zh))
    state[...] = s_cur

    @pl.when(t == n_t - 1)
    def _fin():
        sout_ref[0] = state[...]
        tv = t_valid - (t_valid - 1) // tt * tt
        cout_ref[0] = xbuf[pl.ds(halo + tv - (CONV_W - 1), CONV_W - 1), :]


def _gdn(lin, z, ba, conv_w, cbuf8, s0, alog_l, dtb_l, og, bd, t_valid):
    b, tp, _ = lin.shape
    c = GDN_CHUNK
    nc = GDN_CHUNKS_PER_STEP if tp % (GDN_CHUNKS_PER_STEP * c) == 0 else 1
    tt = nc * c
    n_t = tp // tt
    tile = lambda w: pl.BlockSpec((1, tt, w), lambda bi, t: (bi, t, 0))
    const = lambda shape: pl.BlockSpec(shape, lambda bi, t: (0,) * len(shape))
    return pl.pallas_call(
        functools.partial(_gdn_body, t_valid=t_valid, nc=nc),
        grid=(b, n_t),
        in_specs=[tile(CONV_CH), tile(LIN_W), tile(LANES), const((CONV_W, CONV_CH)),
                  pl.BlockSpec((1, SUBLANES, CONV_CH), lambda bi, t: (bi, 0, 0)),
                  pl.BlockSpec((1, H_LIN, HEAD_DIM, HEAD_DIM), lambda bi, t: (bi, 0, 0, 0)),
                  const((1, LANES)), const((1, LANES)), const((1, HEAD_DIM)), const((LIN_W, LIN_W))],
        out_specs=[tile(LIN_W),
                   pl.BlockSpec((1, H_LIN, HEAD_DIM, HEAD_DIM), lambda bi, t: (bi, 0, 0, 0)),
                   pl.BlockSpec((1, CONV_W - 1, CONV_CH), lambda bi, t: (bi, 0, 0))],
        out_shape=[jax.ShapeDtypeStruct((b, tp, LIN_W), F32),
                   jax.ShapeDtypeStruct((b, H_LIN, HEAD_DIM, HEAD_DIM), F32),
                   jax.ShapeDtypeStruct((b, CONV_W - 1, CONV_CH), F32)],
        scratch_shapes=[pltpu.VMEM((tt + 2 * SUBLANES, CONV_CH), F32),
                        pltpu.VMEM((H_LIN, HEAD_DIM, HEAD_DIM), F32)],
        compiler_params=_cparams(2),
        name="gdn",
    )(lin, z, ba, conv_w, cbuf8, s0, alog_l, dtb_l, og, bd)


def _outproj_body(oa_ref, ol_ref, x_ref, ga_ref, sc_ref, sh_ref, gn_ref, wo_ref, wqt_ref,
                  x1_ref, h2_ref, qt_ref):
    y = (jnp.dot(oa_ref[...].astype(BF16), wo_ref[:ATTN_W, :], preferred_element_type=F32)
         + jnp.dot(ol_ref[...].astype(BF16), wo_ref[ATTN_W:, :], preferred_element_type=F32))
    x1 = x_ref[...] + ga_ref[0] * y
    ms = jnp.mean(x1 * x1, axis=-1, keepdims=True)
    h2 = x1 * lax.rsqrt(ms + EPS) * gn_ref[...]
    h2 = (h2 * (1.0 + sc_ref[0]) + sh_ref[0]).astype(BF16)
    x1_ref[...] = x1
    h2_ref[...] = h2
    qt_ref[...] = lax.dot_general(wqt_ref[...], h2, _NT, preferred_element_type=F32)


def _outproj(oa, ol, x, ga, sc, sh, gn, wo, wqt, tm, tiles_per_mod):
    n, d = x.shape
    r = sc.shape[1]
    pq = wqt.shape[0]
    mod_spec = pl.BlockSpec((1, r, d), lambda i: (i // tiles_per_mod, 0, 0))
    const = lambda shape: pl.BlockSpec(shape, lambda i: (0,) * len(shape))
    row = lambda w: pl.BlockSpec((tm, w), lambda i: (i, 0))
    return pl.pallas_call(
        _outproj_body,
        grid=(n // tm,),
        in_specs=[row(ATTN_W), row(LIN_W), row(d), mod_spec, mod_spec, mod_spec, const((1, d)),
                  const(wo.shape), const(wqt.shape)],
        out_specs=[row(d), row(d), pl.BlockSpec((pq, tm), lambda i: (0, i))],
        out_shape=[jax.ShapeDtypeStruct((n, d), F32), jax.ShapeDtypeStruct((n, d), BF16),
                   jax.ShapeDtypeStruct((pq, n), F32)],
        compiler_params=_cparams(1),
        name="outproj",
    )(oa, ol, x, ga, sc, sh, gn, wo, wqt)


_CAND = [(r1, r2) for r1 in range(PEER_TOPK) for r2 in range(PEER_TOPK) if (r1 + 1) * (r2 + 1) <= PEER_TOPK]
_N_CAND_PAD = _round_up(len(_CAND), 16)
_R1_START = [min(p for p, (r1, _) in enumerate(_CAND) if r1 == r) for r in range(PEER_TOPK)]


def _cand_select_mats():
    m = np.zeros((2, _N_CAND_PAD, PEER_TOPK), np.float32)
    for p, (r1, r2) in enumerate(_CAND):
        m[0, p, r1] = 1.0
        m[1, p, r2] = 1.0
    return jnp.asarray(m, dtype=BF16)


def _topk_rows(s, k):
    rows, t = s.shape
    iota = lax.broadcasted_iota(I32, (rows, t), 0).astype(F32)
    kio = lax.broadcasted_iota(I32, (k, t), 0)
    cur = s
    rank = jnp.full((rows, t), float(k), F32)
    vals = jnp.zeros((k, t), F32)
    pos = jnp.zeros((k, t), F32)
    for j in range(k):
        m = jnp.max(cur, axis=0, keepdims=True)
        idx = jnp.min(jnp.where(cur == m, iota, float(rows)), axis=0, keepdims=True)
        hit = iota == idx
        rank = jnp.where(hit, float(j), rank)
        cur = jnp.where(hit, NEG_INF, cur)
        vals = jnp.where(kio == j, m, vals)
        pos = jnp.where(kio == j, idx, pos)
    return vals, pos, rank


def _route_body(qt_ref, keys_ref, csel_ref, e1_ref, c1_ref, e2_ref, r2_ref):
    kk = PEER_TOPK
    half = keys_ref.shape[3]
    tn = qt_ref.shape[1]
    crow = lax.broadcasted_iota(I32, (_N_CAND_PAD, tn), 0)

    def per_head(h, _):
        s1 = _dot3(keys_ref[h, 0], qt_ref[pl.ds(pl.multiple_of(h * 2 * half, half), half), :])
        s2 = _dot3(keys_ref[h, 1], qt_ref[pl.ds(pl.multiple_of(h * 2 * half + half, half), half), :])
        v1, _, rank1 = _topk_rows(s1, kk)
        v2, _, rank2 = _topk_rows(s2, kk)
        cand = _dot_mask_lhs(csel_ref[0], v1) + _dot_mask_lhs(csel_ref[1], v2)
        cand = jnp.where(crow < len(_CAND), cand, NEG_INF)
        cv, cpos, _ = _topk_rows(cand, kk)
        r1sel = jnp.zeros(cpos.shape, F32)
        for start in _R1_START[1:]:
            r1sel = r1sel + jnp.where(cpos >= float(start), 1.0, 0.0)
        cnt1 = jnp.zeros(s1.shape, F32)
        for j in range(kk):
            cnt1 = cnt1 + jnp.where(rank1 == r1sel[j:j + 1, :], 1.0, 0.0)
        z = jnp.sum(jnp.exp(cv - cv[0:1, :]), axis=0, keepdims=True)
        e1_ref[h] = jnp.where(rank1 < kk, jnp.exp(s1 - v1[0:1, :]), 0.0) / z
        c1_ref[h] = cnt1
        e2_ref[h] = jnp.where(rank2 < kk, jnp.exp(s2 - v2[0:1, :]), 0.0)
        r2_ref[h] = rank2
        return 0

    lax.fori_loop(0, PEER_HEADS, per_head, 0)


def _peer_route(qt, keys, tn):
    pq, n = qt.shape
    csel = _cand_select_mats()
    out = jax.ShapeDtypeStruct((PEER_HEADS, N_KEYS, n), F32)
    ospec = pl.BlockSpec((PEER_HEADS, N_KEYS, tn), lambda i: (0, 0, i))
    return pl.pallas_call(
        _route_body,
        grid=(n // tn,),
        in_specs=[pl.BlockSpec((pq, tn), lambda i: (0, i)),
                  pl.BlockSpec(keys.shape, lambda i: (0, 0, 0, 0)),
                  pl.BlockSpec(csel.shape, lambda i: (0, 0, 0))],
        out_specs=[ospec] * 4,
        out_shape=[out] * 4,
        compiler_params=_cparams(1),
        name="peer_route",
    )(qt, keys, csel)


def _peer_body(h2_ref, u_ref, vt_ref, e1_ref, c1_ref, e2_ref, r2_ref, x1_ref, gf_ref, o_ref,
               acc, ga, *, a_per_tile):
    e = pl.program_id(1)

    @pl.when(e == 0)
    def _zero():
        acc[...] = jnp.zeros(acc.shape, F32)

    for al in range(a_per_tile):
        rows = slice(al * N_KEYS, (al + 1) * N_KEYS)
        pre = lax.dot_general(u_ref[rows, :], h2_ref[...], _NT, preferred_element_type=F32)
        act = 0.5 * pre * (1.0 + lax.erf(pre * (2.0 ** -0.5)))
        g = jnp.zeros(pre.shape, F32)
        for h in range(PEER_HEADS):
            hit = r2_ref[h] < c1_ref[h, al:al + 1, :]
            g = g + jnp.where(hit, e2_ref[h], 0.0) * e1_ref[h, al:al + 1, :]
        ga[rows, :] = (g * act).astype(BF16)
    acc[...] += jnp.dot(vt_ref[...], ga[...], preferred_element_type=F32)

    @pl.when(e == pl.num_programs(1) - 1)
    def _done():
        o_ref[...] = x1_ref[...] + gf_ref[0] * acc[...].T


def _peer_dense(h2, u_bf, vt_bf, e1, c1, e2, r2, x1, gf, tn, te, tiles_per_mod):
    n, d = x1.shape
    n_e = u_bf.shape[0]
    a_per_tile = te // N_KEYS
    r = gf.shape[1]
    key_rows = lambda: pl.BlockSpec((PEER_HEADS, a_per_tile, tn), lambda i, e: (0, e, i))
    key_full = lambda: pl.BlockSpec((PEER_HEADS, N_KEYS, tn), lambda i, e: (0, 0, i))
    return pl.pallas_call(
        functools.partial(_peer_body, a_per_tile=a_per_tile),
        grid=(n // tn, n_e // te),
        in_specs=[pl.BlockSpec((tn, d), lambda i, e: (i, 0)),
                  pl.BlockSpec((te, d), lambda i, e: (e, 0)),
                  pl.BlockSpec((d, te), lambda i, e: (0, e)),
                  key_rows(), key_rows(), key_full(), key_full(),
                  pl.BlockSpec((tn, d), lambda i, e: (i, 0)),
                  pl.BlockSpec((1, r, d), lambda i, e: (i // tiles_per_mod, 0, 0))],
        out_specs=pl.BlockSpec((tn, d), lambda i, e: (i, 0)),
        out_shape=jax.ShapeDtypeStruct((n, d), F32),
        scratch_shapes=[pltpu.VMEM((d, tn), F32), pltpu.VMEM((te, tn), BF16)],
        compiler_params=_cparams(2),
        name="peer_dense",
    )(h2, u_bf, vt_bf, e1, c1, e2, r2, x1, gf)


PEER_EXPERT_TILE = 1024


def _layer_group(x, mods, per_token_mods, p, attend, conv_buf, s0, tm):
    b, t, d = x.shape
    n = b * t
    sh_a, sc_a, g_a, sh_f, sc_f, g_f = mods
    tiles_per_seq = t // tm if not per_token_mods else 1
    tiles_per_mod = n // tm if per_token_mods else tiles_per_seq
    xf = x.reshape(n, d)
    q, k, v, kt, vt, lin, z, ba, km = _inproj(xf, sc_a, sh_a, p["norm_attn"], p["w_in"], p["qg"], p["kg"],
                                              p["bd"], tm, tiles_per_mod, tiles_per_seq)
    o_attn = attend(q.reshape(b, t, ATTN_W), k.reshape(b, t, KV_W), v.reshape(b, t, KV_W), vt, km)
    tp = _round_up(t, GDN_CHUNK)
    pad3 = lambda a: jnp.pad(a.reshape(b, t, -1), ((0, 0), (0, tp - t), (0, 0)))
    cbuf8 = jnp.pad(conv_buf, ((0, 0), (SUBLANES - (CONV_W - 1), 0), (0, 0)))
    o_lin, s_new, conv_new = _gdn(pad3(lin), pad3(z), pad3(ba), p["conv_w"], cbuf8, s0,
                                  p["alog"], p["dtb"], p["og"], p["bd"], t)
    o_lin = o_lin[:, :t].reshape(n, LIN_W)
    x1, h2, qt = _outproj(o_attn.reshape(n, ATTN_W), o_lin, xf, g_a, sc_f, sh_f, p["norm_ffn"],
                          p["w_out"], p["wqt"], tm, tiles_per_mod)
    e1, c1, e2, r2 = _peer_route(qt, p["peer_keys"], tm)
    x2 = _peer_dense(h2, p["u_bf"], p["vt_bf"], e1, c1, e2, r2, x1, g_f, tm, PEER_EXPERT_TILE, tiles_per_mod)
    to_rows = lambda a: a.reshape(-1, N_KV_HEADS, HEAD_DIM, a.shape[-1]).transpose(0, 3, 1, 2)
    return (x2.reshape(b, t, d), to_rows(kt).reshape(b, t, N_KV_HEADS, HEAD_DIM),
            to_rows(vt).reshape(b, t, N_KV_HEADS, HEAD_DIM), s_new, conv_new)


def kernel(x_prompt, x_sample, cache_k, cache_v, state_ssm, state_conv, page_table, c_prompt, c_sample,
           rel_bias, w_ada, b_ada, norm_attn, norm_ffn, w_in, w_out, q_norm, k_norm, conv_w, a_log,
           dt_bias, o_norm, peer_wq, peer_keys, peer_u, peer_v):
    depth = w_ada.shape[0]
    b, t, d = x_prompt.shape
    b2, s, _ = x_sample.shape
    n_pool = cache_k.shape[1]
    n_pages = page_table.shape[1]
    past = n_pages * PAGE_SIZE
    n_blocks = past // MOBA_BLOCK
    s_pad = 16
    assert t % MOBA_BLOCK == 0 and past % MOBA_BLOCK == 0 and n_pages % PAGES_PER_STEP == 0
    assert MOBA_BLOCK + 1 >= MAX_DISTANCE and s <= s_pad and n_blocks <= LANES
    assert w_in.shape[2] + LANES - 2 * H_LIN == D_IN_PAD

    mod = _ada(jnp.concatenate([c_prompt, c_sample], axis=0), w_ada, b_ada)
    btab = _relbias(rel_bias)
    rb_flat = rel_bias.reshape(-1)

    pool_view = lambda c: c.transpose(0, 1, 3, 4, 2).reshape(depth * n_pool, KV_W, PAGE_SIZE)
    ckt, cvt = pool_view(cache_k), pool_view(cache_v)
    bd = (jnp.arange(ATTN_W)[:, None] // HEAD_DIM == jnp.arange(ATTN_W)[None, :] // HEAD_DIM).astype(BF16)

    rows = N_HEADS * s
    bias_last = btab[1, :, :, :s].transpose(0, 2, 1).reshape(rows, MOBA_BLOCK)
    own = btab[0, :, :s_pad, :s].transpose(0, 2, 1)
    causal = jnp.arange(s_pad)[None, :] <= jnp.arange(s)[:, None]
    bias_own = jnp.where(causal[None], own, NEG_INF).reshape(rows, s_pad)
    far = jnp.repeat(rel_bias[N_BUCKETS - 1], s).reshape(rows, 1)
    kv_of_row = jnp.arange(rows) // (s * GQA)
    lane_kv = jnp.arange(KV_W) // HEAD_DIM
    row_mask = (kv_of_row[:, None] == lane_kv[None, :]).astype(F32)
    nbp = _round_up(t // MOBA_BLOCK, SUBLANES)

    xp, xs = x_prompt, x_sample
    outs = [[] for _ in range(8)]
    for l in range(depth):
        lane_pad = lambda a: jnp.pad(a[l][None, :], ((0, 0), (H_LIN, LANES - 2 * H_LIN)))
        p = {
            "norm_attn": norm_attn[l][None, :], "norm_ffn": norm_ffn[l][None, :],
            "w_in": jnp.pad(w_in[l], ((0, 0), (0, D_IN_PAD - w_in.shape[2]))).astype(BF16),
            "w_out": w_out[l].astype(BF16),
            "qg": jnp.tile(q_norm[l], N_HEADS)[None, :], "kg": jnp.tile(k_norm[l], N_KV_HEADS)[None, :],
            "bd": bd, "conv_w": conv_w[l], "alog": lane_pad(a_log), "dtb": lane_pad(dt_bias),
            "og": o_norm[l][None, :], "wqt": peer_wq[l].T.astype(BF16), "peer_keys": peer_keys[l],
            "u_bf": peer_u[l].astype(BF16), "vt_bf": peer_v[l].T.astype(BF16),
        }
        m6 = jnp.split(mod[l], 6, axis=-1)
        mods_p = [m[:b][:, None, :] for m in m6]
        mods_s = [jnp.repeat(m[b:], s, axis=0)[None] for m in m6]

        def attend_p(q, k, v, vt, km):
            kmean = jnp.pad(km.reshape(b, t // MOBA_BLOCK, KV_W), ((0, 0), (0, nbp - t // MOBA_BLOCK), (0, 0)))
            return _moba_prompt(rb_flat, q, k, vt, kmean, btab)

        pt_flat = (page_table + l * n_pool).reshape(-1).astype(I32)

        def attend_s(q, k, v, vt, km):
            qr = q.reshape(b2, s, N_HEADS, HEAD_DIM).transpose(0, 2, 1, 3).reshape(b2, rows, HEAD_DIM)
            qw = jnp.tile(qr, (1, 1, N_KV_HEADS)) * row_mask[None]
            padn = lambda a: jnp.pad(a, ((0, 0), (0, s_pad - s), (0, 0)))
            o = _moba_sample(pt_flat, ckt, cvt, qw, padn(k), padn(v), bias_last, bias_own, far, n_pages)
            return o.reshape(b2, N_HEADS, s, HEAD_DIM).transpose(0, 2, 1, 3).reshape(b2, s, ATTN_W)

        conv0 = jnp.zeros((b, CONV_W - 1, CONV_CH), F32)
        ssm0 = jnp.zeros((b, H_LIN, HEAD_DIM, HEAD_DIM), F32)
        xp, kp, vp, sp, cp = _layer_group(xp, mods_p, False, p, attend_p, conv0, ssm0, MOBA_BLOCK)
        xs, ks, vs, ss, cs = _layer_group(xs, mods_s, True, p, attend_s, state_conv[l], state_ssm[l], b2 * s)
        for lst, val in zip(outs, (kp, vp, ks, vs, sp, ss, cp, cs)):
            lst.append(val)
    return (xp, xs) + tuple(jnp.stack(o) for o in outs)
```

```python
import functools
import math

import numpy as np
import jax
import jax.numpy as jnp
from jax import lax
from jax.experimental import pallas as pl
from jax.experimental.pallas import tpu as pltpu

F32 = jnp.float32
BF16 = jnp.bfloat16
I32 = jnp.int32

HEAD_DIM = 64
N_HEADS = 8
N_KV_HEADS = 4
GQA = N_HEADS // N_KV_HEADS
ATTN_W = N_HEADS * HEAD_DIM
KV_W = N_KV_HEADS * HEAD_DIM
MOBA_BLOCK = 256
MOBA_TOPK = 3
N_BUCKETS = 32
MAX_DISTANCE = 128
H_LIN = 8
LIN_W = H_LIN * HEAD_DIM
CONV_W = 4
CONV_CH = 3 * LIN_W
GDN_CHUNK = 64
GDN_CHUNKS_PER_STEP = 4
INV_BLOCK = 16
PEER_HEADS = 8
N_KEYS = 128
PEER_TOPK = 16
PAGE_SIZE = 128
EPS = 1e-6

LANES = 128
SUBLANES = 8
VMEM_LIMIT_BYTES = 56 * 1024 * 1024

D_IN_PAD = ATTN_W + 2 * KV_W + CONV_CH + LIN_W + LANES
NEG_INF = float("-inf")

_NT = (((1,), (1,)), ((), ()))
_TN = (((0,), (0,)), ((), ()))


def _cparams(n_axes):
    return pltpu.CompilerParams(dimension_semantics=("arbitrary",) * n_axes,
                                vmem_limit_bytes=VMEM_LIMIT_BYTES)


def _round_up(x, m):
    return -(-x // m) * m


def _dot(a, b):
    return jnp.dot(a.astype(BF16), b.astype(BF16), preferred_element_type=F32)


def _dot_nt(a, b):
    return lax.dot_general(a.astype(BF16), b.astype(BF16), _NT, preferred_element_type=F32)


def _dot_tn(a, b):
    return lax.dot_general(a.astype(BF16), b.astype(BF16), _TN, preferred_element_type=F32)


def _bmm(a, b):
    return jnp.einsum("bij,bjk->bik", a.astype(BF16), b.astype(BF16), preferred_element_type=F32)


def _bmm_nt(a, b):
    return jnp.einsum("bik,bjk->bij", a.astype(BF16), b.astype(BF16), preferred_element_type=F32)


def _split2(a):
    hi = a.astype(BF16)
    lo = (a - hi.astype(F32)).astype(BF16)
    return hi, lo


def _split3(a):
    hi = a.astype(BF16)
    r = a - hi.astype(F32)
    mid = r.astype(BF16)
    lo = (r - mid.astype(F32)).astype(BF16)
    return hi, mid, lo


def _dot3(a, b):
    ah, al = _split2(a)
    bh, bl = _split2(b)
    d = functools.partial(jnp.dot, preferred_element_type=F32)
    return d(ah, bh) + (d(al, bh) + d(ah, bl))


def _dot3_nt(a, b):
    ah, al = _split2(a)
    bh, bl = _split2(b)
    d = functools.partial(lax.dot_general, dimension_numbers=_NT, preferred_element_type=F32)
    return d(ah, bh) + (d(al, bh) + d(ah, bl))


def _dot_mask_rhs(a, mask_bf16):
    hi, mid, lo = _split3(a)
    d = functools.partial(jnp.dot, preferred_element_type=F32)
    return d(hi, mask_bf16) + (d(mid, mask_bf16) + d(lo, mask_bf16))


def _dot_mask_lhs(mask_bf16, b):
    hi, mid, lo = _split3(b)
    d = functools.partial(jnp.dot, preferred_element_type=F32)
    return d(mask_bf16, hi) + (d(mask_bf16, mid) + d(mask_bf16, lo))


def _sigmoid(x):
    return 1.0 / (1.0 + jnp.exp(-x))


def _softplus(x):
    return jnp.maximum(x, 0.0) + jnp.log(1.0 + jnp.exp(-jnp.abs(x)))


def _head_sumsq(a, bd):
    return _dot_mask_rhs(a * a, bd)


def _rank_desc(g, idx, n, axis):
    rank = jnp.zeros(g.shape, F32)
    for m in range(n):
        gm = lax.slice_in_dim(g, m, m + 1, axis=axis)
        beats = (gm > g) | ((gm == g) & (idx > m))
        rank = rank + jnp.where(beats, 1.0, 0.0)
    return rank


def _ada_body(c_ref, w_ref, b_ref, o_ref):
    c = c_ref[...]
    s = c * _sigmoid(c)
    o_ref[0] = _dot3(s, w_ref[0]) + b_ref[0]


def _ada(c, w_ada, b_ada):
    n_l, d, d6 = w_ada.shape
    r = c.shape[0]
    tn = 512
    return pl.pallas_call(
        _ada_body,
        grid=(n_l, d6 // tn),
        in_specs=[pl.BlockSpec((r, d), lambda l, j: (0, 0)),
                  pl.BlockSpec((1, d, tn), lambda l, j: (l, 0, j)),
                  pl.BlockSpec((1, 1, tn), lambda l, j: (l, 0, j))],
        out_specs=pl.BlockSpec((1, r, tn), lambda l, j: (l, 0, j)),
        out_shape=jax.ShapeDtypeStruct((n_l, r, d6), F32),
        compiler_params=_cparams(2),
        name="ada",
    )(c, w_ada, b_ada.reshape(n_l, 1, d6))


def _inproj_body(x_ref, sc_ref, sh_ref, gn_ref, w_ref, qg_ref, kg_ref, bd_ref,
                 q_ref, k_ref, v_ref, kt_ref, vt_ref, lin_ref, z_ref, ba_ref, km_ref):
    x = x_ref[...]
    ms = jnp.mean(x * x, axis=-1, keepdims=True)
    h = x * lax.rsqrt(ms + EPS) * gn_ref[...]
    h = h * (1.0 + sc_ref[0]) + sh_ref[0]
    proj = jnp.dot(h.astype(BF16), w_ref[...], preferred_element_type=F32)
    bd = bd_ref[...]
    aq = proj[:, :ATTN_W]
    ak = proj[:, ATTN_W:ATTN_W + KV_W]
    q = aq * lax.rsqrt(_head_sumsq(aq, bd) * (1.0 / HEAD_DIM) + EPS) * qg_ref[...]
    k = ak * lax.rsqrt(_head_sumsq(ak, bd[:KV_W, :KV_W]) * (1.0 / HEAD_DIM) + EPS) * kg_ref[...]
    q_ref[...] = q * (HEAD_DIM ** -0.5)
    k_ref[...] = k
    kt_ref[0] = k.T
    o = ATTN_W + KV_W
    v = proj[:, o:o + KV_W]
    v_ref[...] = v
    vt_ref[0] = v.T
    o += KV_W
    lin_ref[...] = proj[:, o:o + CONV_CH]
    o += CONV_CH
    z_ref[...] = proj[:, o:o + LIN_W]
    o += LIN_W
    ba_ref[...] = proj[:, o:o + LANES]
    km_ref[0] = jnp.mean(k, axis=0, keepdims=True)


def _inproj(x, sc, sh, gn, w_pad, qg, kg, bd, tm, tiles_per_mod, tiles_per_seq):
    n, d = x.shape
    r = sc.shape[1]
    nt = n // tm
    n_seq = nt // tiles_per_seq
    mod_spec = pl.BlockSpec((1, r, d), lambda i: (i // tiles_per_mod, 0, 0))
    const = lambda shape: pl.BlockSpec(shape, lambda i: (0,) * len(shape))
    row = lambda w: pl.BlockSpec((tm, w), lambda i: (i, 0))
    tspec = pl.BlockSpec((1, KV_W, tm), lambda i: (i // tiles_per_seq, 0, i % tiles_per_seq))
    tshape = jax.ShapeDtypeStruct((n_seq, KV_W, tiles_per_seq * tm), F32)
    return pl.pallas_call(
        _inproj_body,
        grid=(nt,),
        in_specs=[row(d), mod_spec, mod_spec, const((1, d)), const(w_pad.shape),
                  const((1, ATTN_W)), const((1, KV_W)), const((ATTN_W, ATTN_W))],
        out_specs=[row(ATTN_W), row(KV_W), row(KV_W), tspec, tspec, row(CONV_CH), row(LIN_W), row(LANES),
                   pl.BlockSpec((1, 1, KV_W), lambda i: (i, 0, 0))],
        out_shape=[jax.ShapeDtypeStruct((n, ATTN_W), F32), jax.ShapeDtypeStruct((n, KV_W), F32),
                   jax.ShapeDtypeStruct((n, KV_W), F32), tshape, tshape,
                   jax.ShapeDtypeStruct((n, CONV_CH), F32),
                   jax.ShapeDtypeStruct((n, LIN_W), F32), jax.ShapeDtypeStruct((n, LANES), F32),
                   jax.ShapeDtypeStruct((nt, 1, KV_W), F32)],
        compiler_params=_cparams(1),
        name="inproj",
    )(x, sc, sh, gn, w_pad, qg, kg, bd)


def _relbias_body(rb_ref, o_ref):
    n_tab, n_h, r, c = o_ref.shape
    key = lax.broadcasted_iota(I32, (r, c), 0)
    qry = lax.broadcasted_iota(I32, (r, c), 1)
    max_exact = N_BUCKETS // 2
    for t in range(n_tab):
        dist = jnp.maximum(qry - key + t * MOBA_BLOCK, 0)
        nf = jnp.maximum(dist, 1).astype(F32)
        large = max_exact + (jnp.log(nf / max_exact) / math.log(MAX_DISTANCE / max_exact)
                             * (N_BUCKETS - max_exact)).astype(I32)
        bucket = jnp.where(dist < max_exact, dist, jnp.minimum(large, N_BUCKETS - 1))

        def per_head(h, _, bucket=bucket, t=t):
            acc = jnp.zeros((r, c), F32)
            for j in range(N_BUCKETS):
                acc = jnp.where(bucket == j, rb_ref[j * n_h + h], acc)
            o_ref[t, h] = acc
            return 0

        lax.fori_loop(0, n_h, per_head, 0)


def _relbias(rel_bias):
    n_h = rel_bias.shape[1]
    return pl.pallas_call(
        _relbias_body,
        in_specs=[pl.BlockSpec(memory_space=pltpu.SMEM)],
        out_specs=pl.BlockSpec(memory_space=pltpu.VMEM),
        out_shape=jax.ShapeDtypeStruct((2, n_h, MOBA_BLOCK, MOBA_BLOCK), F32),
        compiler_params=pltpu.CompilerParams(vmem_limit_bytes=VMEM_LIMIT_BYTES),
        name="relbias",
    )(rel_bias.reshape(-1))


def _moba_prompt_body(rb_ref, q_ref, km_ref, k_ref, vt_ref, bown_ref, bprev_ref, o_ref,
                      selt, m_s, l_s, acct):
    i = pl.program_id(1)
    n = pl.program_id(2)
    nbp = km_ref.shape[1]
    blk = q_ref.shape[1]
    n_steps = pl.num_programs(2)
    far_bucket = N_BUCKETS - 1

    def hs(h):
        return slice(h * HEAD_DIM, (h + 1) * HEAD_DIM)

    def scores(h):
        kv = h // GQA
        return _dot_nt(k_ref[0, :, hs(kv)], q_ref[0, :, hs(h)])

    def vth(h):
        kv = h // GQA
        return vt_ref[0, kv * HEAD_DIM:(kv + 1) * HEAD_DIM, :]

    @pl.when(n == 0)
    def _first():
        row = lax.broadcasted_iota(I32, (nbp, blk), 0)
        kr = lax.broadcasted_iota(I32, (blk, blk), 0)
        qc = lax.broadcasted_iota(I32, (blk, blk), 1)
        causal = kr <= qc
        for h in range(N_HEADS):
            kv = h // GQA
            gate = _dot3_nt(km_ref[0, :, hs(kv)], q_ref[0, :, hs(h)])
            g = jnp.where(row < i, gate, NEG_INF)
            rank = _rank_desc(g, row, nbp, 0)
            selt[h] = jnp.where((rank < MOBA_TOPK) & (row < i), 1.0, 0.0)
            s = jnp.where(causal, scores(h) + bown_ref[h], NEG_INF)
            m = jnp.max(s, axis=0, keepdims=True)
            p = jnp.exp(s - m)
            m_s[h] = m
            l_s[h] = jnp.sum(p, axis=0, keepdims=True)
            acct[hs(h), :] = _dot(vth(h), p)

    @pl.when((n >= 1) & (n <= i))
    def _past():
        kb = n - 1
        is_prev = kb == i - 1
        for h in range(N_HEADS):
            bias = jnp.where(is_prev, bprev_ref[h], rb_ref[far_bucket * N_HEADS + h])
            sel = selt[h, pl.ds(kb, 1), :]
            s = jnp.where(sel > 0.5, scores(h) + bias, NEG_INF)
            m_old = m_s[h]
            m_new = jnp.maximum(m_old, jnp.max(s, axis=0, keepdims=True))
            alpha = jnp.exp(m_old - m_new)
            p = jnp.exp(s - m_new)
            m_s[h] = m_new
            l_s[h] = alpha * l_s[h] + jnp.sum(p, axis=0, keepdims=True)
            acct[hs(h), :] = alpha * acct[hs(h), :] + _dot(vth(h), p)

    @pl.when(n == n_steps - 1)
    def _done():
        for h in range(N_HEADS):
            acct[hs(h), :] = acct[hs(h), :] / l_s[h]
        o_ref[0] = acct[...].T


def _moba_prompt(rb_flat, q, k, vt, kmean, btab):
    b, t, _ = q.shape
    nb = t // MOBA_BLOCK
    nbp = kmean.shape[1]
    blk = MOBA_BLOCK

    def kv_blk(i, n):
        past = jnp.clip(n - 1, 0, jnp.maximum(i - 1, 0))
        return jnp.where(n == 0, i, past)

    return pl.pallas_call(
        _moba_prompt_body,
        grid=(b, nb, nb),
        in_specs=[pl.BlockSpec(memory_space=pltpu.SMEM),
                  pl.BlockSpec((1, blk, ATTN_W), lambda bi, i, n: (bi, i, 0)),
                  pl.BlockSpec((1, nbp, KV_W), lambda bi, i, n: (bi, 0, 0)),
                  pl.BlockSpec((1, blk, KV_W), lambda bi, i, n: (bi, kv_blk(i, n), 0)),
                  pl.BlockSpec((1, KV_W, blk), lambda bi, i, n: (bi, 0, kv_blk(i, n))),
                  pl.BlockSpec((None, N_HEADS, blk, blk), lambda bi, i, n: (0, 0, 0, 0)),
                  pl.BlockSpec((None, N_HEADS, blk, blk), lambda bi, i, n: (1, 0, 0, 0))],
        out_specs=pl.BlockSpec((1, blk, ATTN_W), lambda bi, i, n: (bi, i, 0)),
        out_shape=jax.ShapeDtypeStruct((b, t, ATTN_W), F32),
        scratch_shapes=[pltpu.VMEM((N_HEADS, nbp, blk), F32),
                        pltpu.VMEM((N_HEADS, 1, blk), F32),
                        pltpu.VMEM((N_HEADS, 1, blk), F32),
                        pltpu.VMEM((ATTN_W, blk), F32)],
        compiler_params=_cparams(3),
        name="moba_prompt",
    )(rb_flat, q, kmean, k, vt, btab, btab)


PAGES_PER_STEP = 16
PAGES_PER_BLOCK = MOBA_BLOCK // PAGE_SIZE
BLOCKS_PER_STEP = PAGES_PER_STEP // PAGES_PER_BLOCK


def _moba_sample_body(pt_ref, qw_ref, kn_ref, vn_ref, bl_ref, bo_ref, far_ref, *refs, n_blocks):
    kp = refs[:PAGES_PER_STEP]
    vp = refs[PAGES_PER_STEP:2 * PAGES_PER_STEP]
    o_ref, kres, kmt, selc, m_s, l_s, acc = refs[2 * PAGES_PER_STEP:]
    ph = pl.program_id(1)
    j = pl.program_id(2)
    n_steps = pl.num_programs(2)
    rows = qw_ref.shape[1]
    qw = qw_ref[0]
    qw_b = qw.astype(BF16)

    @pl.when(ph == 0)
    def _scan_keys():
        lane = lax.broadcasted_iota(I32, (KV_W, LANES), 1)

        @pl.when(j == 0)
        def _zero():
            kmt[...] = jnp.zeros(kmt.shape, F32)

        cur = kmt[...]
        for nn in range(BLOCKS_PER_STEP):
            tot = kp[nn * PAGES_PER_BLOCK][0]
            for p in range(1, PAGES_PER_BLOCK):
                tot = tot + kp[nn * PAGES_PER_BLOCK + p][0]
            col = jnp.sum(tot, axis=1, keepdims=True) * (1.0 / MOBA_BLOCK)
            cur = jnp.where(lane == j * BLOCKS_PER_STEP + nn, col, cur)
            for p in range(PAGES_PER_BLOCK):
                r = nn * PAGES_PER_BLOCK + p
                kres[j * PAGES_PER_STEP + r] = kp[r][0].astype(BF16)
        kmt[...] = cur

    @pl.when((ph == 1) & (j == 0))
    def _select():
        gate = _dot3(qw, kmt[...])
        col = lax.broadcasted_iota(I32, (rows, LANES), 1)
        g = jnp.where(col < n_blocks, gate, NEG_INF)
        rank = _rank_desc(g, col, n_blocks, 1)
        sel = jnp.where((rank < MOBA_TOPK) & (col < n_blocks), 1.0, 0.0)
        for jj in range(n_blocks // BLOCKS_PER_STEP):
            selc[jj] = sel[:, jj * BLOCKS_PER_STEP:(jj + 1) * BLOCKS_PER_STEP]
        s = _dot_nt(qw_b, kn_ref[0]) + bo_ref[...]
        m = jnp.max(s, axis=-1, keepdims=True)
        p = jnp.exp(s - m)
        m_s[...] = m
        l_s[...] = jnp.sum(p, axis=-1, keepdims=True)
        acc[...] = _dot(p, vn_ref[0])

    @pl.when(ph == 1)
    def _attend():
        sel_j = selc[j]
        pieces = []
        for nn in range(BLOCKS_PER_STEP):
            s = jnp.concatenate(
                [jnp.dot(qw_b, kres[j * PAGES_PER_STEP + nn * PAGES_PER_BLOCK + p],
                         preferred_element_type=F32) for p in range(PAGES_PER_BLOCK)], axis=1)
            if nn == BLOCKS_PER_STEP - 1:
                s = s + jnp.where(j == n_steps - 1, bl_ref[...], far_ref[...])
            else:
                s = s + far_ref[...]
            pieces.append(jnp.where(sel_j[:, nn:nn + 1] > 0.5, s, NEG_INF))
        s_all = jnp.concatenate(pieces, axis=1)
        m_old = m_s[...]
        m_new = jnp.maximum(m_old, jnp.max(s_all, axis=-1, keepdims=True))
        alpha = jnp.exp(m_old - m_new)
        p_all = jnp.exp(s_all - m_new)
        m_s[...] = m_new
        l_s[...] = alpha * l_s[...] + jnp.sum(p_all, axis=-1, keepdims=True)
        p_b = p_all.astype(BF16)
        pv = jnp.zeros(acc.shape, F32)
        for r in range(PAGES_PER_STEP):
            pv = pv + lax.dot_general(p_b[:, r * PAGE_SIZE:(r + 1) * PAGE_SIZE], vp[r][0].astype(BF16),
                                      _NT, preferred_element_type=F32)
        acc[...] = alpha * acc[...] + pv

    @pl.when((ph == 1) & (j == n_steps - 1))
    def _done():
        out = acc[...] / l_s[...]
        rpk = rows // N_KV_HEADS
        for kv in range(N_KV_HEADS):
            o_ref[0, kv * rpk:(kv + 1) * rpk, :] = out[kv * rpk:(kv + 1) * rpk,
                                                      kv * HEAD_DIM:(kv + 1) * HEAD_DIM]


def _moba_sample(pt_flat, ckt, cvt, qw, kn, vn, bias_last, bias_own, far, n_pages):
    b2, rows, _ = qw.shape
    n_blocks = n_pages // PAGES_PER_BLOCK
    n_steps = n_pages // PAGES_PER_STEP
    s_pad = kn.shape[1]
    last = n_steps - 1

    def k_spec(r):
        return pl.BlockSpec((1, KV_W, PAGE_SIZE), lambda b, ph, j, pt: (
            pt[b * n_pages + jnp.where(ph == 0, j, last) * PAGES_PER_STEP + r], 0, 0))

    def v_spec(r):
        return pl.BlockSpec((1, KV_W, PAGE_SIZE), lambda b, ph, j, pt: (
            pt[b * n_pages + jnp.where(ph == 0, 0, j) * PAGES_PER_STEP + r], 0, 0))

    per_b = lambda shape: pl.BlockSpec((1,) + shape, lambda b, ph, j, pt: (b, 0, 0))
    const = lambda shape: pl.BlockSpec(shape, lambda b, ph, j, pt: (0, 0))
    return pl.pallas_call(
        functools.partial(_moba_sample_body, n_blocks=n_blocks),
        grid_spec=pltpu.PrefetchScalarGridSpec(
            num_scalar_prefetch=1,
            grid=(b2, 2, n_steps),
            in_specs=[per_b((rows, KV_W)), per_b((s_pad, KV_W)), per_b((s_pad, KV_W)),
                      const((rows, MOBA_BLOCK)), const((rows, s_pad)), const((rows, 1))]
            + [k_spec(r) for r in range(PAGES_PER_STEP)] + [v_spec(r) for r in range(PAGES_PER_STEP)],
            out_specs=pl.BlockSpec((1, rows, HEAD_DIM), lambda b, ph, j, pt: (b, 0, 0)),
            scratch_shapes=[pltpu.VMEM((n_pages, KV_W, PAGE_SIZE), BF16),
                            pltpu.VMEM((KV_W, LANES), F32),
                            pltpu.VMEM((n_steps, rows, BLOCKS_PER_STEP), F32),
                            pltpu.VMEM((rows, 1), F32),
                            pltpu.VMEM((rows, 1), F32),
                            pltpu.VMEM((rows, KV_W), F32)]),
        out_shape=jax.ShapeDtypeStruct((b2, rows, HEAD_DIM), F32),
        compiler_params=_cparams(3),
        name="moba_sample",
    )(pt_flat, qw, kn, vn, bias_last, bias_own, far,
      *([ckt] * PAGES_PER_STEP), *([cvt] * PAGES_PER_STEP))


def _gdn_body(lin_ref, z_ref, ba_ref, cw_ref, cb_ref, s0_ref, al_ref, dtb_ref, og_ref, bd_ref,
              o_ref, sout_ref, cout_ref, xbuf, state, *, t_valid, nc):
    t = pl.program_id(1)
    n_t = pl.num_programs(1)
    c = GDN_CHUNK
    tt = nc * c
    halo = SUBLANES

    @pl.when(t == 0)
    def _init():
        xbuf[0:halo, :] = cb_ref[0]
        state[...] = s0_ref[0]

    @pl.when(t > 0)
    def _carry():
        xbuf[0:halo, :] = xbuf[tt:tt + halo, :]

    xbuf[halo:halo + tt, :] = lin_ref[0]
    first = halo - (CONV_W - 1)
    conv = cw_ref[0:1, :] * xbuf[pl.ds(first, tt), :]
    for i in range(1, CONV_W):
        conv = conv + cw_ref[i:i + 1, :] * xbuf[pl.ds(first + i, tt), :]
    a = conv * _sigmoid(conv)
    lq = a[:, :LIN_W]
    lk = a[:, LIN_W:2 * LIN_W]
    lv = a[:, 2 * LIN_W:]
    bd = bd_ref[...]
    lq = lq * lax.rsqrt(_head_sumsq(lq, bd) + EPS) * (HEAD_DIM ** -0.5)
    lk = lk * lax.rsqrt(_head_sumsq(lk, bd) + EPS)
    ba = ba_ref[0]
    beta_all = _sigmoid(ba)
    g_all = -jnp.exp(al_ref[...]) * _softplus(ba + dtb_ref[...])
    if t_valid % tt != 0:
        rid = t * tt + lax.broadcasted_iota(I32, (tt, 1), 0)
        ok = rid < t_valid
        lq = jnp.where(ok, lq, 0.0)
        lk = jnp.where(ok, lk, 0.0)
        lv = jnp.where(ok, lv, 0.0)
        beta_all = jnp.where(ok, beta_all, 0.0)
        g_all = jnp.where(ok, g_all, 0.0)

    it = lax.broadcasted_iota(I32, (tt, tt), 0)
    jt = lax.broadcasted_iota(I32, (tt, tt), 1)
    tril = jnp.where((it >= jt) & (it // c == jt // c), 1.0, 0.0).astype(BF16)
    gc_all = _dot_mask_lhs(tril, g_all)

    def heads(x):
        return jnp.stack([x[ci * c:(ci + 1) * c, h * HEAD_DIM:(h + 1) * HEAD_DIM]
                          for ci in range(nc) for h in range(H_LIN)], axis=0)

    def cols(x, off):
        return jnp.stack([x[ci * c:(ci + 1) * c, off + h:off + h + 1]
                          for ci in range(nc) for h in range(H_LIN)], axis=0)

    q, k, v = heads(lq), heads(lk), heads(lv)
    beta = cols(beta_all, 0)
    gc = cols(gc_all, H_LIN)
    ii = lax.broadcasted_iota(I32, (1, c, c), 1)
    jj = lax.broadcasted_iota(I32, (1, c, c), 2)
    lower = ii >= jj
    strict = ii > jj
    diag_blk = (ii // INV_BLOCK) == (jj // INV_BLOCK)
    gc_row = jnp.sum(jnp.where(ii == jj, gc, 0.0), axis=1, keepdims=True)
    eg = jnp.exp(gc)
    dmat = jnp.where(lower, jnp.exp(jnp.minimum(gc - gc_row, 0.0)), 0.0)
    kb = k * beta
    vb = v * beta
    lm = _bmm_nt(kb, k) * jnp.where(strict, dmat, 0.0)
    dm = jnp.where(diag_blk, lm, 0.0)
    nm = lm - dm
    xp = -dm
    p = _bmm(dm, dm)
    steps = int(math.log2(INV_BLOCK)) - 1
    for s in range(steps):
        xp = xp + p + _bmm(xp, p)
        if s + 1 < steps:
            p = _bmm(p, p)
    mm = nm + _bmm(xp, nm)
    yp = -mm
    pm = _bmm(mm, mm)
    msteps = int(math.log2(c // INV_BLOCK)) - 1
    for s in range(msteps):
        yp = yp + pm + _bmm(yp, pm)
        if s + 1 < msteps:
            pm = _bmm(pm, pm)
    tp = yp + xp + _bmm(yp, xp)
    kbg = kb * eg
    u = vb + _bmm(tp, vb)
    w = kbg + _bmm(tp, kbg)
    qk = _bmm_nt(q, k) * dmat
    qd = q * eg
    gl = gc[:, c - 1:c, :]
    kt = k * jnp.exp(gl - gc)
    egl = jnp.exp(gl)

    s_cur = state[...]
    for ci in range(nc):
        sl = slice(ci * H_LIN, (ci + 1) * H_LIN)
        v_new = u[sl] - _bmm(w[sl], s_cur)
        o = _bmm(qd[sl], s_cur) + _bmm(qk[sl], v_new)
        upd = jnp.stack([_dot_tn(kt[ci * H_LIN + h], v_new[h]) for h in range(H_LIN)], axis=0)
        s_cur = s_cur * egl[sl] + upd
        on = o * lax.rsqrt(jnp.mean(o * o, axis=-1, keepdims=True) + EPS) * og_ref[...]
        for h in range(H_LIN):
            zh = z_ref[0, ci * c:(ci + 1) * c, h * HEAD_DIM:(h + 1) * HEAD_DIM]
            o_ref[0, ci * c:(ci + 1) * c, h * HEAD_DIM:(h + 1) * HEAD_DIM] = on[h] * (zh * _sigmoid(zh))
    state[...] = s_cur

    @pl.when(t == n_t - 1)
    def _fin():
        sout_ref[0] = state[...]
        tv = t_valid - (t_valid - 1) // tt * tt
        cout_ref[0] = xbuf[pl.ds(halo + tv - (CONV_W - 1), CONV_W - 1), :]


def _gdn(lin, z, ba, conv_w, cbuf8, s0, alog_l, dtb_l, og, bd, t_valid):
    b, tp, _ = lin.shape
    c = GDN_CHUNK
    nc = GDN_CHUNKS_PER_STEP if tp % (GDN_CHUNKS_PER_STEP * c) == 0 else 1
    tt = nc * c
    n_t = tp // tt
    tile = lambda w: pl.BlockSpec((1, tt, w), lambda bi, t: (bi, t, 0))
    const = lambda shape: pl.BlockSpec(shape, lambda bi, t: (0,) * len(shape))
    return pl.pallas_call(
        functools.partial(_gdn_body, t_valid=t_valid, nc=nc),
        grid=(b, n_t),
        in_specs=[tile(CONV_CH), tile(LIN_W), tile(LANES), const((CONV_W, CONV_CH)),
                  pl.BlockSpec((1, SUBLANES, CONV_CH), lambda bi, t: (bi, 0, 0)),
                  pl.BlockSpec((1, H_LIN, HEAD_DIM, HEAD_DIM), lambda bi, t: (bi, 0, 0, 0)),
                  const((1, LANES)), const((1, LANES)), const((1, HEAD_DIM)), const((LIN_W, LIN_W))],
        out_specs=[tile(LIN_W),
                   pl.BlockSpec((1, H_LIN, HEAD_DIM, HEAD_DIM), lambda bi, t: (bi, 0, 0, 0)),
                   pl.BlockSpec((1, CONV_W - 1, CONV_CH), lambda bi, t: (bi, 0, 0))],
        out_shape=[jax.ShapeDtypeStruct((b, tp, LIN_W), F32),
                   jax.ShapeDtypeStruct((b, H_LIN, HEAD_DIM, HEAD_DIM), F32),
                   jax.ShapeDtypeStruct((b, CONV_W - 1, CONV_CH), F32)],
        scratch_shapes=[pltpu.VMEM((tt + 2 * SUBLANES, CONV_CH), F32),
                        pltpu.VMEM((H_LIN, HEAD_DIM, HEAD_DIM), F32)],
        compiler_params=_cparams(2),
        name="gdn",
    )(lin, z, ba, conv_w, cbuf8, s0, alog_l, dtb_l, og, bd)


def _outproj_body(oa_ref, ol_ref, x_ref, ga_ref, sc_ref, sh_ref, gn_ref, wo_ref, wqt_ref,
                  x1_ref, h2_ref, qt_ref):
    y = (jnp.dot(oa_ref[...].astype(BF16), wo_ref[:ATTN_W, :], preferred_element_type=F32)
         + jnp.dot(ol_ref[...].astype(BF16), wo_ref[ATTN_W:, :], preferred_element_type=F32))
    x1 = x_ref[...] + ga_ref[0] * y
    ms = jnp.mean(x1 * x1, axis=-1, keepdims=True)
    h2 = x1 * lax.rsqrt(ms + EPS) * gn_ref[...]
    h2 = (h2 * (1.0 + sc_ref[0]) + sh_ref[0]).astype(BF16)
    x1_ref[...] = x1
    h2_ref[...] = h2
    qt_ref[...] = lax.dot_general(wqt_ref[...], h2, _NT, preferred_element_type=F32)


def _outproj(oa, ol, x, ga, sc, sh, gn, wo, wqt, tm, tiles_per_mod):
    n, d = x.shape
    r = sc.shape[1]
    pq = wqt.shape[0]
    mod_spec = pl.BlockSpec((1, r, d), lambda i: (i // tiles_per_mod, 0, 0))
    const = lambda shape: pl.BlockSpec(shape, lambda i: (0,) * len(shape))
    row = lambda w: pl.BlockSpec((tm, w), lambda i: (i, 0))
    return pl.pallas_call(
        _outproj_body,
        grid=(n // tm,),
        in_specs=[row(ATTN_W), row(LIN_W), row(d), mod_spec, mod_spec, mod_spec, const((1, d)),
                  const(wo.shape), const(wqt.shape)],
        out_specs=[row(d), row(d), pl.BlockSpec((pq, tm), lambda i: (0, i))],
        out_shape=[jax.ShapeDtypeStruct((n, d), F32), jax.ShapeDtypeStruct((n, d), BF16),
                   jax.ShapeDtypeStruct((pq, n), F32)],
        compiler_params=_cparams(1),
        name="outproj",
    )(oa, ol, x, ga, sc, sh, gn, wo, wqt)


_CAND = [(r1, r2) for r1 in range(PEER_TOPK) for r2 in range(PEER_TOPK) if (r1 + 1) * (r2 + 1) <= PEER_TOPK]
_N_CAND_PAD = _round_up(len(_CAND), 16)
_R1_START = [min(p for p, (r1, _) in enumerate(_CAND) if r1 == r) for r in range(PEER_TOPK)]


def _cand_select_mats():
    m = np.zeros((2, _N_CAND_PAD, PEER_TOPK), np.float32)
    for p, (r1, r2) in enumerate(_CAND):
        m[0, p, r1] = 1.0
        m[1, p, r2] = 1.0
    return jnp.asarray(m, dtype=BF16)


def _topk_rows(s, k):
    rows, t = s.shape
    iota = lax.broadcasted_iota(I32, (rows, t), 0).astype(F32)
    kio = lax.broadcasted_iota(I32, (k, t), 0)
    cur = s
    rank = jnp.full((rows, t), float(k), F32)
    vals = jnp.zeros((k, t), F32)
    pos = jnp.zeros((k, t), F32)
    for j in range(k):
        m = jnp.max(cur, axis=0, keepdims=True)
        idx = jnp.min(jnp.where(cur == m, iota, float(rows)), axis=0, keepdims=True)
        hit = iota == idx
        rank = jnp.where(hit, float(j), rank)
        cur = jnp.where(hit, NEG_INF, cur)
        vals = jnp.where(kio == j, m, vals)
        pos = jnp.where(kio == j, idx, pos)
    return vals, pos, rank


def _route_body(qt_ref, keys_ref, csel_ref, e1_ref, c1_ref, e2_ref, r2_ref):
    kk = PEER_TOPK
    half = keys_ref.shape[3]
    tn = qt_ref.shape[1]
    crow = lax.broadcasted_iota(I32, (_N_CAND_PAD, tn), 0)
    kio = lax.broadcasted_iota(I32, (kk, tn), 0).astype(F32)
    krow = lax.broadcasted_iota(I32, (N_KEYS, tn), 0).astype(F32)

    def per_head(h, _):
        s1 = _dot3(keys_ref[h, 0], qt_ref[pl.ds(pl.multiple_of(h * 2 * half, half), half), :])
        s2 = _dot3(keys_ref[h, 1], qt_ref[pl.ds(pl.multiple_of(h * 2 * half + half, half), half), :])
        v1, pos1, rank1 = _topk_rows(s1, kk)
        v2, _, rank2 = _topk_rows(s2, kk)
        cand = _dot_mask_lhs(csel_ref[0], v1) + _dot_mask_lhs(csel_ref[1], v2)
        cand = jnp.where(crow < len(_CAND), cand, NEG_INF)
        cv, cpos, _ = _topk_rows(cand, kk)
        r1sel = jnp.zeros(cpos.shape, F32)
        for start in _R1_START[1:]:
            r1sel = r1sel + jnp.where(cpos >= float(start), 1.0, 0.0)
        per_rank = jnp.zeros((kk, tn), F32)
        for j in range(kk):
            per_rank = per_rank + jnp.where(kio == r1sel[j:j + 1, :], 1.0, 0.0)
        cnt1 = jnp.zeros(s1.shape, F32)
        for j in range(kk):
            cnt1 = jnp.where(krow == pos1[j:j + 1, :], per_rank[j:j + 1, :], cnt1)
        z = jnp.sum(jnp.exp(cv - cv[0:1, :]), axis=0, keepdims=True)
        e1_ref[h] = jnp.where(rank1 < kk, jnp.exp(s1 - v1[0:1, :]), 0.0) / z
        c1_ref[h] = cnt1
        e2_ref[h] = jnp.where(rank2 < kk, jnp.exp(s2 - v2[0:1, :]), 0.0).astype(BF16)
        r2_ref[h] = rank2.astype(BF16)
        return 0

    lax.fori_loop(0, PEER_HEADS, per_head, 0)


def _peer_route(qt, keys, tn):
    pq, n = qt.shape
    csel = _cand_select_mats()
    out = lambda dt: jax.ShapeDtypeStruct((PEER_HEADS, N_KEYS, n), dt)
    ospec = pl.BlockSpec((PEER_HEADS, N_KEYS, tn), lambda i: (0, 0, i))
    return pl.pallas_call(
        _route_body,
        grid=(n // tn,),
        in_specs=[pl.BlockSpec((pq, tn), lambda i: (0, i)),
                  pl.BlockSpec(keys.shape, lambda i: (0, 0, 0, 0)),
                  pl.BlockSpec(csel.shape, lambda i: (0, 0, 0))],
        out_specs=[ospec] * 4,
        out_shape=[out(F32), out(F32), out(BF16), out(BF16)],
        compiler_params=_cparams(1),
        name="peer_route",
    )(qt, keys, csel)


BF16_ROWS = 2 * SUBLANES


def _peer_body(h2_ref, u_ref, vt_ref, e1_ref, c1_ref, e2_ref, r2_ref, x1_ref, gf_ref, o_ref,
               acc, pre_s, ga, *, a_per_tile):
    e = pl.program_id(1)
    tn = h2_ref.shape[0]

    @pl.when(e == 0)
    def _zero():
        acc[...] = jnp.zeros(acc.shape, F32)

    pre_s[...] = lax.dot_general(u_ref[...], h2_ref[...], _NT, preferred_element_type=F32)
    for al in range(a_per_tile):
        g = jnp.zeros((N_KEYS // BF16_ROWS, BF16_ROWS, tn), BF16)
        for h in range(PEER_HEADS):
            cnt = jnp.broadcast_to(c1_ref[h, al:al + 1, :], (BF16_ROWS, tn)).astype(BF16)
            w1 = jnp.broadcast_to(e1_ref[h, al:al + 1, :], (BF16_ROWS, tn)).astype(BF16)
            hit = r2_ref[h] < cnt[None]
            g = g + jnp.where(hit, e2_ref[h], jnp.zeros((), BF16)) * w1[None]
        ga[al * N_KEYS:(al + 1) * N_KEYS, :] = g.reshape(N_KEYS, tn)
    n_part = 2
    part = a_per_tile * N_KEYS // n_part
    for pi in range(n_part):
        prows = slice(pi * part, (pi + 1) * part)
        pre = pre_s[prows, :]
        act = 0.5 * pre * (1.0 + lax.erf(pre * (2.0 ** -0.5)))
        gact = ga[prows, :] * act.astype(BF16)
        acc[...] += jnp.dot(vt_ref[:, prows], gact, preferred_element_type=F32)

    @pl.when(e == pl.num_programs(1) - 1)
    def _done():
        o_ref[...] = x1_ref[...] + gf_ref[0] * acc[...].T


def _peer_dense(h2, u_bf, vt_bf, e1, c1, e2, r2, x1, gf, tn, te, tiles_per_mod):
    n, d = x1.shape
    n_e = u_bf.shape[0]
    a_per_tile = te // N_KEYS
    r = gf.shape[1]
    packed = lambda a: a.reshape(PEER_HEADS, N_KEYS // BF16_ROWS, BF16_ROWS, n)
    key_rows = lambda: pl.BlockSpec((PEER_HEADS, a_per_tile, tn), lambda i, e: (0, e, i))
    key_full = lambda: pl.BlockSpec((PEER_HEADS, N_KEYS // BF16_ROWS, BF16_ROWS, tn), lambda i, e: (0, 0, 0, i))
    return pl.pallas_call(
        functools.partial(_peer_body, a_per_tile=a_per_tile),
        grid=(n // tn, n_e // te),
        in_specs=[pl.BlockSpec((tn, d), lambda i, e: (i, 0)),
                  pl.BlockSpec((te, d), lambda i, e: (e, 0)),
                  pl.BlockSpec((d, te), lambda i, e: (0, e)),
                  key_rows(), key_rows(), key_full(), key_full(),
                  pl.BlockSpec((tn, d), lambda i, e: (i, 0)),
                  pl.BlockSpec((1, r, d), lambda i, e: (i // tiles_per_mod, 0, 0))],
        out_specs=pl.BlockSpec((tn, d), lambda i, e: (i, 0)),
        out_shape=jax.ShapeDtypeStruct((n, d), F32),
        scratch_shapes=[pltpu.VMEM((d, tn), F32), pltpu.VMEM((te, tn), F32), pltpu.VMEM((te, tn), BF16)],
        compiler_params=_cparams(2),
        name="peer_dense",
    )(h2, u_bf, vt_bf, e1, c1, packed(e2), packed(r2), x1, gf)


PEER_EXPERT_TILE = 1024
PEER_TOKEN_TILE = 512


def _layer_group(x, mods, per_token_mods, p, attend, conv_buf, s0, tm, tn_peer):
    b, t, d = x.shape
    n = b * t
    sh_a, sc_a, g_a, sh_f, sc_f, g_f = mods
    tiles_per_seq = t // tm if not per_token_mods else 1
    tiles_per_mod = n // tm if per_token_mods else tiles_per_seq
    xf = x.reshape(n, d)
    q, k, v, kt, vt, lin, z, ba, km = _inproj(xf, sc_a, sh_a, p["norm_attn"], p["w_in"], p["qg"], p["kg"],
                                              p["bd"], tm, tiles_per_mod, tiles_per_seq)
    o_attn = attend(q.reshape(b, t, ATTN_W), k.reshape(b, t, KV_W), v.reshape(b, t, KV_W), vt, km)
    tp = _round_up(t, GDN_CHUNK)
    pad3 = lambda a: jnp.pad(a.reshape(b, t, -1), ((0, 0), (0, tp - t), (0, 0)))
    cbuf8 = jnp.pad(conv_buf, ((0, 0), (SUBLANES - (CONV_W - 1), 0), (0, 0)))
    o_lin, s_new, conv_new = _gdn(pad3(lin), pad3(z), pad3(ba), p["conv_w"], cbuf8, s0,
                                  p["alog"], p["dtb"], p["og"], p["bd"], t)
    o_lin = o_lin[:, :t].reshape(n, LIN_W)
    x1, h2, qt = _outproj(o_attn.reshape(n, ATTN_W), o_lin, xf, g_a, sc_f, sh_f, p["norm_ffn"],
                          p["w_out"], p["wqt"], tm, tiles_per_mod)
    e1, c1, e2, r2 = _peer_route(qt, p["peer_keys"], tm)
    x2 = _peer_dense(h2, p["u_bf"], p["vt_bf"], e1, c1, e2, r2, x1, g_f, tn_peer, PEER_EXPERT_TILE,
                     n // tn_peer if per_token_mods else t // tn_peer)
    to_rows = lambda a: a.reshape(-1, N_KV_HEADS, HEAD_DIM, a.shape[-1]).transpose(0, 3, 1, 2)
    return (x2.reshape(b, t, d), to_rows(kt).reshape(b, t, N_KV_HEADS, HEAD_DIM),
            to_rows(vt).reshape(b, t, N_KV_HEADS, HEAD_DIM), s_new, conv_new)


def kernel(x_prompt, x_sample, cache_k, cache_v, state_ssm, state_conv, page_table, c_prompt, c_sample,
           rel_bias, w_ada, b_ada, norm_attn, norm_ffn, w_in, w_out, q_norm, k_norm, conv_w, a_log,
           dt_bias, o_norm, peer_wq, peer_keys, peer_u, peer_v):
    depth = w_ada.shape[0]
    b, t, d = x_prompt.shape
    b2, s, _ = x_sample.shape
    n_pool = cache_k.shape[1]
    n_pages = page_table.shape[1]
    past = n_pages * PAGE_SIZE
    n_blocks = past // MOBA_BLOCK
    s_pad = 16
    assert t % MOBA_BLOCK == 0 and past % MOBA_BLOCK == 0 and n_pages % PAGES_PER_STEP == 0
    assert MOBA_BLOCK + 1 >= MAX_DISTANCE and s <= s_pad and n_blocks <= LANES
    assert w_in.shape[2] + LANES - 2 * H_LIN == D_IN_PAD

    mod = _ada(jnp.concatenate([c_prompt, c_sample], axis=0), w_ada, b_ada)
    btab = _relbias(rel_bias)
    rb_flat = rel_bias.reshape(-1)

    pool_view = lambda c: c.transpose(0, 1, 3, 4, 2).reshape(depth * n_pool, KV_W, PAGE_SIZE)
    ckt, cvt = pool_view(cache_k), pool_view(cache_v)
    bd = (jnp.arange(ATTN_W)[:, None] // HEAD_DIM == jnp.arange(ATTN_W)[None, :] // HEAD_DIM).astype(BF16)

    rows = N_HEADS * s
    bias_last = btab[1, :, :, :s].transpose(0, 2, 1).reshape(rows, MOBA_BLOCK)
    own = btab[0, :, :s_pad, :s].transpose(0, 2, 1)
    causal = jnp.arange(s_pad)[None, :] <= jnp.arange(s)[:, None]
    bias_own = jnp.where(causal[None], own, NEG_INF).reshape(rows, s_pad)
    far = jnp.repeat(rel_bias[N_BUCKETS - 1], s).reshape(rows, 1)
    kv_of_row = jnp.arange(rows) // (s * GQA)
    lane_kv = jnp.arange(KV_W) // HEAD_DIM
    row_mask = (kv_of_row[:, None] == lane_kv[None, :]).astype(F32)
    nbp = _round_up(t // MOBA_BLOCK, SUBLANES)

    xp, xs = x_prompt, x_sample
    outs = [[] for _ in range(8)]
    for l in range(depth):
        lane_pad = lambda a: jnp.pad(a[l][None, :], ((0, 0), (H_LIN, LANES - 2 * H_LIN)))
        p = {
            "norm_attn": norm_attn[l][None, :], "norm_ffn": norm_ffn[l][None, :],
            "w_in": jnp.pad(w_in[l], ((0, 0), (0, D_IN_PAD - w_in.shape[2]))).astype(BF16),
            "w_out": w_out[l].astype(BF16),
            "qg": jnp.tile(q_norm[l], N_HEADS)[None, :], "kg": jnp.tile(k_norm[l], N_KV_HEADS)[None, :],
            "bd": bd, "conv_w": conv_w[l], "alog": lane_pad(a_log), "dtb": lane_pad(dt_bias),
            "og": o_norm[l][None, :], "wqt": peer_wq[l].T.astype(BF16), "peer_keys": peer_keys[l],
            "u_bf": peer_u[l].astype(BF16), "vt_bf": peer_v[l].T.astype(BF16),
        }
        m6 = jnp.split(mod[l], 6, axis=-1)
        mods_p = [m[:b][:, None, :] for m in m6]
        mods_s = [jnp.repeat(m[b:], s, axis=0)[None] for m in m6]

        def attend_p(q, k, v, vt, km):
            kmean = jnp.pad(km.reshape(b, t // MOBA_BLOCK, KV_W), ((0, 0), (0, nbp - t // MOBA_BLOCK), (0, 0)))
            return _moba_prompt(rb_flat, q, k, vt, kmean, btab)

        pt_flat = (page_table + l * n_pool).reshape(-1).astype(I32)

        def attend_s(q, k, v, vt, km):
            qr = q.reshape(b2, s, N_HEADS, HEAD_DIM).transpose(0, 2, 1, 3).reshape(b2, rows, HEAD_DIM)
            qw = jnp.tile(qr, (1, 1, N_KV_HEADS)) * row_mask[None]
            padn = lambda a: jnp.pad(a, ((0, 0), (0, s_pad - s), (0, 0)))
            o = _moba_sample(pt_flat, ckt, cvt, qw, padn(k), padn(v), bias_last, bias_own, far, n_pages)
            return o.reshape(b2, N_HEADS, s, HEAD_DIM).transpose(0, 2, 1, 3).reshape(b2, s, ATTN_W)

        conv0 = jnp.zeros((b, CONV_W - 1, CONV_CH), F32)
        ssm0 = jnp.zeros((b, H_LIN, HEAD_DIM, HEAD_DIM), F32)
        xp, kp, vp, sp, cp = _layer_group(xp, mods_p, False, p, attend_p, conv0, ssm0, MOBA_BLOCK, PEER_TOKEN_TILE)
        xs, ks, vs, ss, cs = _layer_group(xs, mods_s, True, p, attend_s, state_conv[l], state_ssm[l], b2 * s, b2 * s)
        for lst, val in zip(outs, (kp, vp, ks, vs, sp, ss, cp, cs)):
            lst.append(val)
    return (xp, xs) + tuple(jnp.stack(o) for o in outs)
```

```python
import functools
import math

import numpy as np
import jax
import jax.numpy as jnp
from jax import lax
from jax.experimental import pallas as pl
from jax.experimental.pallas import tpu as pltpu

F32 = jnp.float32
BF16 = jnp.bfloat16
I32 = jnp.int32

HEAD_DIM = 64
N_HEADS = 8
N_KV_HEADS = 4
GQA = N_HEADS // N_KV_HEADS
ATTN_W = N_HEADS * HEAD_DIM
KV_W = N_KV_HEADS * HEAD_DIM
MOBA_BLOCK = 256
MOBA_TOPK = 3
N_BUCKETS = 32
MAX_DISTANCE = 128
H_LIN = 8
LIN_W = H_LIN * HEAD_DIM
CONV_W = 4
CONV_CH = 3 * LIN_W
GDN_CHUNK = 64
GDN_CHUNKS_PER_STEP = 4
INV_BLOCK = 16
PEER_HEADS = 8
N_KEYS = 128
PEER_TOPK = 16
PAGE_SIZE = 128
EPS = 1e-6

LANES = 128
SUBLANES = 8
VMEM_LIMIT_BYTES = 56 * 1024 * 1024

D_IN_PAD = ATTN_W + 2 * KV_W + CONV_CH + LIN_W + LANES
NEG_INF = float("-inf")

_NT = (((1,), (1,)), ((), ()))
_TN = (((0,), (0,)), ((), ()))


def _cparams(n_axes):
    return pltpu.CompilerParams(dimension_semantics=("arbitrary",) * n_axes,
                                vmem_limit_bytes=VMEM_LIMIT_BYTES)


def _round_up(x, m):
    return -(-x // m) * m


def _dot(a, b):
    return jnp.dot(a.astype(BF16), b.astype(BF16), preferred_element_type=F32)


def _dot_nt(a, b):
    return lax.dot_general(a.astype(BF16), b.astype(BF16), _NT, preferred_element_type=F32)


def _dot_tn(a, b):
    return lax.dot_general(a.astype(BF16), b.astype(BF16), _TN, preferred_element_type=F32)


def _bmm(a, b):
    return jnp.einsum("bij,bjk->bik", a.astype(BF16), b.astype(BF16), preferred_element_type=F32)


def _bmm_nt(a, b):
    return jnp.einsum("bik,bjk->bij", a.astype(BF16), b.astype(BF16), preferred_element_type=F32)


def _split2(a):
    hi = a.astype(BF16)
    lo = (a - hi.astype(F32)).astype(BF16)
    return hi, lo


def _split3(a):
    hi = a.astype(BF16)
    r = a - hi.astype(F32)
    mid = r.astype(BF16)
    lo = (r - mid.astype(F32)).astype(BF16)
    return hi, mid, lo


def _dot3(a, b):
    ah, al = _split2(a)
    bh, bl = _split2(b)
    d = functools.partial(jnp.dot, preferred_element_type=F32)
    return d(ah, bh) + (d(al, bh) + d(ah, bl))


def _dot3_nt(a, b):
    ah, al = _split2(a)
    bh, bl = _split2(b)
    d = functools.partial(lax.dot_general, dimension_numbers=_NT, preferred_element_type=F32)
    return d(ah, bh) + (d(al, bh) + d(ah, bl))


def _dot_mask_rhs(a, mask_bf16):
    hi, mid, lo = _split3(a)
    d = functools.partial(jnp.dot, preferred_element_type=F32)
    return d(hi, mask_bf16) + (d(mid, mask_bf16) + d(lo, mask_bf16))


def _dot_mask_lhs(mask_bf16, b):
    hi, mid, lo = _split3(b)
    d = functools.partial(jnp.dot, preferred_element_type=F32)
    return d(mask_bf16, hi) + (d(mask_bf16, mid) + d(mask_bf16, lo))


def _sigmoid(x):
    return 1.0 / (1.0 + jnp.exp(-x))


def _softplus(x):
    return jnp.maximum(x, 0.0) + jnp.log(1.0 + jnp.exp(-jnp.abs(x)))


def _head_sumsq(a, bd):
    return _dot_mask_rhs(a * a, bd)


def _rank_desc(g, idx, n, axis):
    rank = jnp.zeros(g.shape, F32)
    for m in range(n):
        gm = lax.slice_in_dim(g, m, m + 1, axis=axis)
        beats = (gm > g) | ((gm == g) & (idx > m))
        rank = rank + jnp.where(beats, 1.0, 0.0)
    return rank


def _ada_body(c_ref, w_ref, b_ref, o_ref):
    c = c_ref[...]
    s = c * _sigmoid(c)
    o_ref[0] = _dot3(s, w_ref[0]) + b_ref[0]


def _ada(c, w_ada, b_ada):
    n_l, d, d6 = w_ada.shape
    r = c.shape[0]
    tn = 512
    return pl.pallas_call(
        _ada_body,
        grid=(n_l, d6 // tn),
        in_specs=[pl.BlockSpec((r, d), lambda l, j: (0, 0)),
                  pl.BlockSpec((1, d, tn), lambda l, j: (l, 0, j)),
                  pl.BlockSpec((1, 1, tn), lambda l, j: (l, 0, j))],
        out_specs=pl.BlockSpec((1, r, tn), lambda l, j: (l, 0, j)),
        out_shape=jax.ShapeDtypeStruct((n_l, r, d6), F32),
        compiler_params=_cparams(2),
        name="ada",
    )(c, w_ada, b_ada.reshape(n_l, 1, d6))


def _inproj_body(x_ref, sc_ref, sh_ref, gn_ref, w_ref, qg_ref, kg_ref, bd_ref,
                 q_ref, k_ref, v_ref, kt_ref, vt_ref, lin_ref, z_ref, ba_ref, km_ref):
    x = x_ref[...]
    ms = jnp.mean(x * x, axis=-1, keepdims=True)
    h = x * lax.rsqrt(ms + EPS) * gn_ref[...]
    h = h * (1.0 + sc_ref[0]) + sh_ref[0]
    proj = jnp.dot(h.astype(BF16), w_ref[...], preferred_element_type=F32)
    bd = bd_ref[...]
    aq = proj[:, :ATTN_W]
    ak = proj[:, ATTN_W:ATTN_W + KV_W]
    q = aq * lax.rsqrt(_head_sumsq(aq, bd) * (1.0 / HEAD_DIM) + EPS) * qg_ref[...]
    k = ak * lax.rsqrt(_head_sumsq(ak, bd[:KV_W, :KV_W]) * (1.0 / HEAD_DIM) + EPS) * kg_ref[...]
    q_ref[...] = q * (HEAD_DIM ** -0.5)
    k_ref[...] = k
    kt_ref[0] = k.T
    o = ATTN_W + KV_W
    v = proj[:, o:o + KV_W]
    v_ref[...] = v
    vt_ref[0] = v.T
    o += KV_W
    lin_ref[...] = proj[:, o:o + CONV_CH]
    o += CONV_CH
    z_ref[...] = proj[:, o:o + LIN_W]
    o += LIN_W
    ba_ref[...] = proj[:, o:o + LANES]
    km_ref[0] = jnp.mean(k, axis=0, keepdims=True)


def _inproj(x, sc, sh, gn, w_pad, qg, kg, bd, tm, tiles_per_mod, tiles_per_seq):
    n, d = x.shape
    r = sc.shape[1]
    nt = n // tm
    n_seq = nt // tiles_per_seq
    mod_spec = pl.BlockSpec((1, r, d), lambda i: (i // tiles_per_mod, 0, 0))
    const = lambda shape: pl.BlockSpec(shape, lambda i: (0,) * len(shape))
    row = lambda w: pl.BlockSpec((tm, w), lambda i: (i, 0))
    tspec = pl.BlockSpec((1, KV_W, tm), lambda i: (i // tiles_per_seq, 0, i % tiles_per_seq))
    tshape = jax.ShapeDtypeStruct((n_seq, KV_W, tiles_per_seq * tm), F32)
    return pl.pallas_call(
        _inproj_body,
        grid=(nt,),
        in_specs=[row(d), mod_spec, mod_spec, const((1, d)), const(w_pad.shape),
                  const((1, ATTN_W)), const((1, KV_W)), const((ATTN_W, ATTN_W))],
        out_specs=[row(ATTN_W), row(KV_W), row(KV_W), tspec, tspec, row(CONV_CH), row(LIN_W), row(LANES),
                   pl.BlockSpec((1, 1, KV_W), lambda i: (i, 0, 0))],
        out_shape=[jax.ShapeDtypeStruct((n, ATTN_W), F32), jax.ShapeDtypeStruct((n, KV_W), F32),
                   jax.ShapeDtypeStruct((n, KV_W), F32), tshape, tshape,
                   jax.ShapeDtypeStruct((n, CONV_CH), F32),
                   jax.ShapeDtypeStruct((n, LIN_W), F32), jax.ShapeDtypeStruct((n, LANES), F32),
                   jax.ShapeDtypeStruct((nt, 1, KV_W), F32)],
        compiler_params=_cparams(1),
        name="inproj",
    )(x, sc, sh, gn, w_pad, qg, kg, bd)


def _relbias_body(rb_ref, o_ref):
    n_tab, n_h, r, c = o_ref.shape
    key = lax.broadcasted_iota(I32, (r, c), 0)
    qry = lax.broadcasted_iota(I32, (r, c), 1)
    max_exact = N_BUCKETS // 2
    for t in range(n_tab):
        dist = jnp.maximum(qry - key + t * MOBA_BLOCK, 0)
        nf = jnp.maximum(dist, 1).astype(F32)
        large = max_exact + (jnp.log(nf / max_exact) / math.log(MAX_DISTANCE / max_exact)
                             * (N_BUCKETS - max_exact)).astype(I32)
        bucket = jnp.where(dist < max_exact, dist, jnp.minimum(large, N_BUCKETS - 1))

        def per_head(h, _, bucket=bucket, t=t):
            acc = jnp.zeros((r, c), F32)
            for j in range(N_BUCKETS):
                acc = jnp.where(bucket == j, rb_ref[j * n_h + h], acc)
            o_ref[t, h] = acc
            return 0

        lax.fori_loop(0, n_h, per_head, 0)


def _relbias(rel_bias):
    n_h = rel_bias.shape[1]
    return pl.pallas_call(
        _relbias_body,
        in_specs=[pl.BlockSpec(memory_space=pltpu.SMEM)],
        out_specs=pl.BlockSpec(memory_space=pltpu.VMEM),
        out_shape=jax.ShapeDtypeStruct((2, n_h, MOBA_BLOCK, MOBA_BLOCK), F32),
        compiler_params=pltpu.CompilerParams(vmem_limit_bytes=VMEM_LIMIT_BYTES),
        name="relbias",
    )(rel_bias.reshape(-1))


def _moba_prompt_body(sch_ref, rb_ref, q_ref, km_ref, k_ref, vt_ref, bown_ref, bprev_ref, o_ref,
                      selt, m_s, l_s, acct):
    pair = pl.program_id(1)
    i = sch_ref[pair]
    n = sch_ref[pl.num_programs(1) + pair]
    nbp = km_ref.shape[1]
    blk = q_ref.shape[1]
    far_bucket = N_BUCKETS - 1

    def hs(h):
        return slice(h * HEAD_DIM, (h + 1) * HEAD_DIM)

    def scores(h):
        kv = h // GQA
        return _dot_nt(k_ref[0, :, hs(kv)], q_ref[0, :, hs(h)])

    def vth(h):
        kv = h // GQA
        return vt_ref[0, kv * HEAD_DIM:(kv + 1) * HEAD_DIM, :]

    @pl.when(n == 0)
    def _first():
        row = lax.broadcasted_iota(I32, (nbp, blk), 0)
        kr = lax.broadcasted_iota(I32, (blk, blk), 0)
        qc = lax.broadcasted_iota(I32, (blk, blk), 1)
        causal = kr <= qc
        for h in range(N_HEADS):
            kv = h // GQA
            gate = _dot3_nt(km_ref[0, :, hs(kv)], q_ref[0, :, hs(h)])
            g = jnp.where(row < i, gate, NEG_INF)
            rank = _rank_desc(g, row, nbp, 0)
            selt[h] = jnp.where((rank < MOBA_TOPK) & (row < i), 1.0, 0.0)
            s = jnp.where(causal, scores(h) + bown_ref[h], NEG_INF)
            m = jnp.max(s, axis=0, keepdims=True)
            p = jnp.exp(s - m)
            m_s[h] = m
            l_s[h] = jnp.sum(p, axis=0, keepdims=True)
            acct[hs(h), :] = _dot(vth(h), p)

    @pl.when(n >= 1)
    def _past():
        kb = n - 1
        is_prev = kb == i - 1
        for h in range(N_HEADS):
            bias = jnp.where(is_prev, bprev_ref[h], rb_ref[far_bucket * N_HEADS + h])
            sel = selt[h, pl.ds(kb, 1), :]
            s = jnp.where(sel > 0.5, scores(h) + bias, NEG_INF)
            m_old = m_s[h]
            m_new = jnp.maximum(m_old, jnp.max(s, axis=0, keepdims=True))
            alpha = jnp.exp(m_old - m_new)
            p = jnp.exp(s - m_new)
            m_s[h] = m_new
            l_s[h] = alpha * l_s[h] + jnp.sum(p, axis=0, keepdims=True)
            acct[hs(h), :] = alpha * acct[hs(h), :] + _dot(vth(h), p)

    @pl.when(n == i)
    def _done():
        for h in range(N_HEADS):
            acct[hs(h), :] = acct[hs(h), :] / l_s[h]
        o_ref[0] = acct[...].T


def _moba_prompt(rb_flat, q, k, vt, kmean, btab):
    b, t, _ = q.shape
    nb = t // MOBA_BLOCK
    nbp = kmean.shape[1]
    blk = MOBA_BLOCK

    pairs = [(i, n) for i in range(nb) for n in range(i + 1)]
    n_pairs = len(pairs)
    sched = jnp.asarray([p[0] for p in pairs] + [p[1] for p in pairs], I32)

    def q_blk(pr, sch):
        return sch[pr]

    def kv_blk(pr, sch):
        i, n = sch[pr], sch[n_pairs + pr]
        return jnp.where(n == 0, i, n - 1)

    return pl.pallas_call(
        _moba_prompt_body,
        grid_spec=pltpu.PrefetchScalarGridSpec(
            num_scalar_prefetch=1,
            grid=(b, n_pairs),
            in_specs=[pl.BlockSpec(memory_space=pltpu.SMEM),
                      pl.BlockSpec((1, blk, ATTN_W), lambda bi, pr, sch: (bi, q_blk(pr, sch), 0)),
                      pl.BlockSpec((1, nbp, KV_W), lambda bi, pr, sch: (bi, 0, 0)),
                      pl.BlockSpec((1, blk, KV_W), lambda bi, pr, sch: (bi, kv_blk(pr, sch), 0)),
                      pl.BlockSpec((1, KV_W, blk), lambda bi, pr, sch: (bi, 0, kv_blk(pr, sch))),
                      pl.BlockSpec((None, N_HEADS, blk, blk), lambda bi, pr, sch: (0, 0, 0, 0)),
                      pl.BlockSpec((None, N_HEADS, blk, blk), lambda bi, pr, sch: (1, 0, 0, 0))],
            out_specs=pl.BlockSpec((1, blk, ATTN_W), lambda bi, pr, sch: (bi, q_blk(pr, sch), 0)),
            scratch_shapes=[pltpu.VMEM((N_HEADS, nbp, blk), F32),
                            pltpu.VMEM((N_HEADS, 1, blk), F32),
                            pltpu.VMEM((N_HEADS, 1, blk), F32),
                            pltpu.VMEM((ATTN_W, blk), F32)]),
        out_shape=jax.ShapeDtypeStruct((b, t, ATTN_W), F32),
        compiler_params=_cparams(2),
        name="moba_prompt",
    )(sched, rb_flat, q, kmean, k, vt, btab, btab)


PAGES_PER_STEP = 32
PAGES_PER_BLOCK = MOBA_BLOCK // PAGE_SIZE
BLOCKS_PER_STEP = PAGES_PER_STEP // PAGES_PER_BLOCK


def _moba_sample_body(pt_ref, qw_ref, kn_ref, vn_ref, bl_ref, bo_ref, far_ref, *refs, n_blocks):
    kp = refs[:PAGES_PER_STEP]
    vp = refs[PAGES_PER_STEP:2 * PAGES_PER_STEP]
    o_ref, kres, kmt, selc, m_s, l_s, acc = refs[2 * PAGES_PER_STEP:]
    ph = pl.program_id(1)
    j = pl.program_id(2)
    n_steps = pl.num_programs(2)
    rows = qw_ref.shape[1]
    qw = qw_ref[0]
    qw_b = qw.astype(BF16)

    @pl.when(ph == 0)
    def _scan_keys():
        lane = lax.broadcasted_iota(I32, (KV_W, LANES), 1)

        @pl.when(j == 0)
        def _zero():
            kmt[...] = jnp.zeros(kmt.shape, F32)

        cur = kmt[...]
        for nn in range(BLOCKS_PER_STEP):
            tot = kp[nn * PAGES_PER_BLOCK][0]
            for p in range(1, PAGES_PER_BLOCK):
                tot = tot + kp[nn * PAGES_PER_BLOCK + p][0]
            col = jnp.sum(tot, axis=1, keepdims=True) * (1.0 / MOBA_BLOCK)
            cur = jnp.where(lane == j * BLOCKS_PER_STEP + nn, col, cur)
            for p in range(PAGES_PER_BLOCK):
                r = nn * PAGES_PER_BLOCK + p
                kres[j * PAGES_PER_STEP + r] = kp[r][0].astype(BF16)
        kmt[...] = cur

    @pl.when((ph == 1) & (j == 0))
    def _select():
        gate = _dot3(qw, kmt[...])
        col = lax.broadcasted_iota(I32, (rows, LANES), 1)
        g = jnp.where(col < n_blocks, gate, NEG_INF)
        rank = _rank_desc(g, col, n_blocks, 1)
        sel = jnp.where((rank < MOBA_TOPK) & (col < n_blocks), 1.0, 0.0)
        for jj in range(n_blocks // BLOCKS_PER_STEP):
            selc[jj] = sel[:, jj * BLOCKS_PER_STEP:(jj + 1) * BLOCKS_PER_STEP]
        s = _dot_nt(qw_b, kn_ref[0]) + bo_ref[...]
        m = jnp.max(s, axis=-1, keepdims=True)
        p = jnp.exp(s - m)
        m_s[...] = m
        l_s[...] = jnp.sum(p, axis=-1, keepdims=True)
        acc[...] = _dot(p, vn_ref[0])

    @pl.when(ph == 1)
    def _attend():
        sel_j = selc[j]
        pieces = []
        for nn in range(BLOCKS_PER_STEP):
            s = jnp.concatenate(
                [jnp.dot(qw_b, kres[j * PAGES_PER_STEP + nn * PAGES_PER_BLOCK + p],
                         preferred_element_type=F32) for p in range(PAGES_PER_BLOCK)], axis=1)
            if nn == BLOCKS_PER_STEP - 1:
                s = s + jnp.where(j == n_steps - 1, bl_ref[...], far_ref[...])
            else:
                s = s + far_ref[...]
            pieces.append(jnp.where(sel_j[:, nn:nn + 1] > 0.5, s, NEG_INF))
        s_all = jnp.concatenate(pieces, axis=1)
        m_old = m_s[...]
        m_new = jnp.maximum(m_old, jnp.max(s_all, axis=-1, keepdims=True))
        alpha = jnp.exp(m_old - m_new)
        p_all = jnp.exp(s_all - m_new)
        m_s[...] = m_new
        l_s[...] = alpha * l_s[...] + jnp.sum(p_all, axis=-1, keepdims=True)
        p_b = p_all.astype(BF16)
        pv = jnp.zeros(acc.shape, F32)
        for r in range(PAGES_PER_STEP):
            pv = pv + lax.dot_general(p_b[:, r * PAGE_SIZE:(r + 1) * PAGE_SIZE], vp[r][0].astype(BF16),
                                      _NT, preferred_element_type=F32)
        acc[...] = alpha * acc[...] + pv

    @pl.when((ph == 1) & (j == n_steps - 1))
    def _done():
        out = acc[...] / l_s[...]
        rpk = rows // N_KV_HEADS
        for kv in range(N_KV_HEADS):
            o_ref[0, kv * rpk:(kv + 1) * rpk, :] = out[kv * rpk:(kv + 1) * rpk,
                                                      kv * HEAD_DIM:(kv + 1) * HEAD_DIM]


def _moba_sample(pt_flat, ckt, cvt, qw, kn, vn, bias_last, bias_own, far, n_pages):
    b2, rows, _ = qw.shape
    n_blocks = n_pages // PAGES_PER_BLOCK
    n_steps = n_pages // PAGES_PER_STEP
    s_pad = kn.shape[1]
    last = n_steps - 1

    def k_spec(r):
        return pl.BlockSpec((1, KV_W, PAGE_SIZE), lambda b, ph, j, pt: (
            pt[b * n_pages + jnp.where(ph == 0, j, last) * PAGES_PER_STEP + r], 0, 0))

    def v_spec(r):
        return pl.BlockSpec((1, KV_W, PAGE_SIZE), lambda b, ph, j, pt: (
            pt[b * n_pages + jnp.where(ph == 0, 0, j) * PAGES_PER_STEP + r], 0, 0))

    per_b = lambda shape: pl.BlockSpec((1,) + shape, lambda b, ph, j, pt: (b, 0, 0))
    const = lambda shape: pl.BlockSpec(shape, lambda b, ph, j, pt: (0, 0))
    return pl.pallas_call(
        functools.partial(_moba_sample_body, n_blocks=n_blocks),
        grid_spec=pltpu.PrefetchScalarGridSpec(
            num_scalar_prefetch=1,
            grid=(b2, 2, n_steps),
            in_specs=[per_b((rows, KV_W)), per_b((s_pad, KV_W)), per_b((s_pad, KV_W)),
                      const((rows, MOBA_BLOCK)), const((rows, s_pad)), const((rows, 1))]
            + [k_spec(r) for r in range(PAGES_PER_STEP)] + [v_spec(r) for r in range(PAGES_PER_STEP)],
            out_specs=pl.BlockSpec((1, rows, HEAD_DIM), lambda b, ph, j, pt: (b, 0, 0)),
            scratch_shapes=[pltpu.VMEM((n_pages, KV_W, PAGE_SIZE), BF16),
                            pltpu.VMEM((KV_W, LANES), F32),
                            pltpu.VMEM((n_steps, rows, BLOCKS_PER_STEP), F32),
                            pltpu.VMEM((rows, 1), F32),
                            pltpu.VMEM((rows, 1), F32),
                            pltpu.VMEM((rows, KV_W), F32)]),
        out_shape=jax.ShapeDtypeStruct((b2, rows, HEAD_DIM), F32),
        compiler_params=_cparams(3),
        name="moba_sample",
    )(pt_flat, qw, kn, vn, bias_last, bias_own, far,
      *([ckt] * PAGES_PER_STEP), *([cvt] * PAGES_PER_STEP))


def _gdn_body(lin_ref, z_ref, ba_ref, cw_ref, cb_ref, s0_ref, al_ref, dtb_ref, og_ref, bd_ref,
              o_ref, sout_ref, cout_ref, xbuf, state, *, t_valid, nc):
    t = pl.program_id(1)
    n_t = pl.num_programs(1)
    c = GDN_CHUNK
    tt = nc * c
    halo = SUBLANES

    @pl.when(t == 0)
    def _init():
        xbuf[0:halo, :] = cb_ref[0]
        state[...] = s0_ref[0]

    @pl.when(t > 0)
    def _carry():
        xbuf[0:halo, :] = xbuf[tt:tt + halo, :]

    xbuf[halo:halo + tt, :] = lin_ref[0]
    first = halo - (CONV_W - 1)
    conv = cw_ref[0:1, :] * xbuf[pl.ds(first, tt), :]
    for i in range(1, CONV_W):
        conv = conv + cw_ref[i:i + 1, :] * xbuf[pl.ds(first + i, tt), :]
    a = conv * _sigmoid(conv)
    lq = a[:, :LIN_W]
    lk = a[:, LIN_W:2 * LIN_W]
    lv = a[:, 2 * LIN_W:]
    bd = bd_ref[...]
    lq = lq * lax.rsqrt(_head_sumsq(lq, bd) + EPS) * (HEAD_DIM ** -0.5)
    lk = lk * lax.rsqrt(_head_sumsq(lk, bd) + EPS)
    ba = ba_ref[0]
    beta_all = _sigmoid(ba)
    g_all = -jnp.exp(al_ref[...]) * _softplus(ba + dtb_ref[...])
    if t_valid % tt != 0:
        rid = t * tt + lax.broadcasted_iota(I32, (tt, 1), 0)
        ok = rid < t_valid
        lq = jnp.where(ok, lq, 0.0)
        lk = jnp.where(ok, lk, 0.0)
        lv = jnp.where(ok, lv, 0.0)
        beta_all = jnp.where(ok, beta_all, 0.0)
        g_all = jnp.where(ok, g_all, 0.0)

    it = lax.broadcasted_iota(I32, (tt, tt), 0)
    jt = lax.broadcasted_iota(I32, (tt, tt), 1)
    tril = jnp.where((it >= jt) & (it // c == jt // c), 1.0, 0.0).astype(BF16)
    gc_all = _dot_mask_lhs(tril, g_all)

    def heads(x):
        return jnp.stack([x[ci * c:(ci + 1) * c, h * HEAD_DIM:(h + 1) * HEAD_DIM]
                          for ci in range(nc) for h in range(H_LIN)], axis=0)

    def cols(x, off):
        return jnp.stack([x[ci * c:(ci + 1) * c, off + h:off + h + 1]
                          for ci in range(nc) for h in range(H_LIN)], axis=0)

    q, k, v = heads(lq), heads(lk), heads(lv)
    beta = cols(beta_all, 0)
    gc = cols(gc_all, H_LIN)
    ii = lax.broadcasted_iota(I32, (1, c, c), 1)
    jj = lax.broadcasted_iota(I32, (1, c, c), 2)
    lower = ii >= jj
    strict = ii > jj
    diag_blk = (ii // INV_BLOCK) == (jj // INV_BLOCK)
    gc_row = jnp.sum(jnp.where(ii == jj, gc, 0.0), axis=1, keepdims=True)
    eg = jnp.exp(gc)
    dmat = jnp.where(lower, jnp.exp(jnp.minimum(gc - gc_row, 0.0)), 0.0)
    kb = k * beta
    vb = v * beta
    lm = _bmm_nt(kb, k) * jnp.where(strict, dmat, 0.0)
    dm = jnp.where(diag_blk, lm, 0.0)
    nm = lm - dm
    xp = -dm
    p = _bmm(dm, dm)
    steps = int(math.log2(INV_BLOCK)) - 1
    for s in range(steps):
        xp = xp + p + _bmm(xp, p)
        if s + 1 < steps:
            p = _bmm(p, p)
    mm = nm + _bmm(xp, nm)
    yp = -mm
    pm = _bmm(mm, mm)
    msteps = int(math.log2(c // INV_BLOCK)) - 1
    for s in range(msteps):
        yp = yp + pm + _bmm(yp, pm)
        if s + 1 < msteps:
            pm = _bmm(pm, pm)
    tp = yp + xp + _bmm(yp, xp)
    kbg = kb * eg
    u = vb + _bmm(tp, vb)
    w = kbg + _bmm(tp, kbg)
    qk = _bmm_nt(q, k) * dmat
    qd = q * eg
    gl = gc[:, c - 1:c, :]
    kt = k * jnp.exp(gl - gc)
    egl = jnp.exp(gl)

    s_cur = state[...]
    for ci in range(nc):
        sl = slice(ci * H_LIN, (ci + 1) * H_LIN)
        v_new = u[sl] - _bmm(w[sl], s_cur)
        o = _bmm(qd[sl], s_cur) + _bmm(qk[sl], v_new)
        upd = jnp.stack([_dot_tn(kt[ci * H_LIN + h], v_new[h]) for h in range(H_LIN)], axis=0)
        s_cur = s_cur * egl[sl] + upd
        on = o * lax.rsqrt(jnp.mean(o * o, axis=-1, keepdims=True) + EPS) * og_ref[...]
        for h in range(H_LIN):
            zh = z_ref[0, ci * c:(ci + 1) * c, h * HEAD_DIM:(h + 1) * HEAD_DIM]
            o_ref[0, ci * c:(ci + 1) * c, h * HEAD_DIM:(h + 1) * HEAD_DIM] = on[h] * (zh * _sigmoid(zh))
    state[...] = s_cur

    @pl.when(t == n_t - 1)
    def _fin():
        sout_ref[0] = state[...]
        tv = t_valid - (t_valid - 1) // tt * tt
        cout_ref[0] = xbuf[pl.ds(halo + tv - (CONV_W - 1), CONV_W - 1), :]


def _gdn(lin, z, ba, conv_w, cbuf8, s0, alog_l, dtb_l, og, bd, t_valid):
    b, tp, _ = lin.shape
    c = GDN_CHUNK
    nc = GDN_CHUNKS_PER_STEP if tp % (GDN_CHUNKS_PER_STEP * c) == 0 else 1
    tt = nc * c
    n_t = tp // tt
    tile = lambda w: pl.BlockSpec((1, tt, w), lambda bi, t: (bi, t, 0))
    const = lambda shape: pl.BlockSpec(shape, lambda bi, t: (0,) * len(shape))
    return pl.pallas_call(
        functools.partial(_gdn_body, t_valid=t_valid, nc=nc),
        grid=(b, n_t),
        in_specs=[tile(CONV_CH), tile(LIN_W), tile(LANES), const((CONV_W, CONV_CH)),
                  pl.BlockSpec((1, SUBLANES, CONV_CH), lambda bi, t: (bi, 0, 0)),
                  pl.BlockSpec((1, H_LIN, HEAD_DIM, HEAD_DIM), lambda bi, t: (bi, 0, 0, 0)),
                  const((1, LANES)), const((1, LANES)), const((1, HEAD_DIM)), const((LIN_W, LIN_W))],
        out_specs=[tile(LIN_W),
                   pl.BlockSpec((1, H_LIN, HEAD_DIM, HEAD_DIM), lambda bi, t: (bi, 0, 0, 0)),
                   pl.BlockSpec((1, CONV_W - 1, CONV_CH), lambda bi, t: (bi, 0, 0))],
        out_shape=[jax.ShapeDtypeStruct((b, tp, LIN_W), F32),
                   jax.ShapeDtypeStruct((b, H_LIN, HEAD_DIM, HEAD_DIM), F32),
                   jax.ShapeDtypeStruct((b, CONV_W - 1, CONV_CH), F32)],
        scratch_shapes=[pltpu.VMEM((tt + 2 * SUBLANES, CONV_CH), F32),
                        pltpu.VMEM((H_LIN, HEAD_DIM, HEAD_DIM), F32)],
        compiler_params=_cparams(2),
        name="gdn",
    )(lin, z, ba, conv_w, cbuf8, s0, alog_l, dtb_l, og, bd)


def _outproj_body(oa_ref, ol_ref, x_ref, ga_ref, sc_ref, sh_ref, gn_ref, wo_ref, wqt_ref,
                  x1_ref, h2_ref, qt_ref):
    y = (jnp.dot(oa_ref[...].astype(BF16), wo_ref[:ATTN_W, :], preferred_element_type=F32)
         + jnp.dot(ol_ref[...].astype(BF16), wo_ref[ATTN_W:, :], preferred_element_type=F32))
    x1 = x_ref[...] + ga_ref[0] * y
    ms = jnp.mean(x1 * x1, axis=-1, keepdims=True)
    h2 = x1 * lax.rsqrt(ms + EPS) * gn_ref[...]
    h2 = (h2 * (1.0 + sc_ref[0]) + sh_ref[0]).astype(BF16)
    x1_ref[...] = x1
    h2_ref[...] = h2
    qt_ref[...] = lax.dot_general(wqt_ref[...], h2, _NT, preferred_element_type=F32)


def _outproj(oa, ol, x, ga, sc, sh, gn, wo, wqt, tm, tiles_per_mod):
    n, d = x.shape
    r = sc.shape[1]
    pq = wqt.shape[0]
    mod_spec = pl.BlockSpec((1, r, d), lambda i: (i // tiles_per_mod, 0, 0))
    const = lambda shape: pl.BlockSpec(shape, lambda i: (0,) * len(shape))
    row = lambda w: pl.BlockSpec((tm, w), lambda i: (i, 0))
    return pl.pallas_call(
        _outproj_body,
        grid=(n // tm,),
        in_specs=[row(ATTN_W), row(LIN_W), row(d), mod_spec, mod_spec, mod_spec, const((1, d)),
                  const(wo.shape), const(wqt.shape)],
        out_specs=[row(d), row(d), pl.BlockSpec((pq, tm), lambda i: (0, i))],
        out_shape=[jax.ShapeDtypeStruct((n, d), F32), jax.ShapeDtypeStruct((n, d), BF16),
                   jax.ShapeDtypeStruct((pq, n), F32)],
        compiler_params=_cparams(1),
        name="outproj",
    )(oa, ol, x, ga, sc, sh, gn, wo, wqt)


_CAND = [(r1, r2) for r1 in range(PEER_TOPK) for r2 in range(PEER_TOPK) if (r1 + 1) * (r2 + 1) <= PEER_TOPK]
_N_CAND_PAD = _round_up(len(_CAND), 16)


def _cand_select_mats():
    m = np.zeros((2, _N_CAND_PAD, PEER_TOPK), np.float32)
    for p, (r1, r2) in enumerate(_CAND):
        m[0, p, r1] = 1.0
        m[1, p, r2] = 1.0
    return jnp.asarray(m, dtype=BF16)


def _topk_rows(s, k, tie_exact):
    rows, t = s.shape
    kio = lax.broadcasted_iota(I32, (k, t), 0)
    cur = s
    rank = jnp.full((rows, t), float(k), F32)
    vals = jnp.zeros((k, t), F32)
    if tie_exact:
        iota = lax.broadcasted_iota(I32, (rows, t), 0).astype(F32)
    for j in range(k):
        m = jnp.max(cur, axis=0, keepdims=True)
        hit = cur == m
        if tie_exact:
            idx = jnp.min(jnp.where(hit, iota, float(rows)), axis=0, keepdims=True)
            hit = iota == idx
        rank = jnp.where(hit, float(j), rank)
        cur = jnp.where(hit, NEG_INF, cur)
        vals = jnp.where(kio == j, m, vals)
    removed = jnp.sum(jnp.where(rank < k, 1.0, 0.0), axis=0, keepdims=True)
    return vals, rank, removed


def _route_head(s1, s2, csel_ref, tie_exact):
    kk = PEER_TOPK
    tn = s1.shape[1]
    crow = lax.broadcasted_iota(I32, (_N_CAND_PAD, tn), 0)
    v1, rank1, n1 = _topk_rows(s1, kk, tie_exact)
    v2, rank2, n2 = _topk_rows(s2, kk, tie_exact)
    cand = _dot_mask_lhs(csel_ref[0], v1) + _dot_mask_lhs(csel_ref[1], v2)
    cand = jnp.where(crow < len(_CAND), cand, NEG_INF)
    cv, crank, nc = _topk_rows(cand, kk, tie_exact)
    per_rank = _dot_tn(csel_ref[0], jnp.where(crank < kk, 1.0, 0.0))
    cnt1 = jnp.zeros(s1.shape, F32)
    for j in range(kk):
        cnt1 = jnp.where(rank1 == float(j), per_rank[j:j + 1, :], cnt1)
    z = jnp.sum(jnp.exp(cv - cv[0:1, :]), axis=0, keepdims=True)
    w1 = jnp.where(rank1 < kk, jnp.exp(s1 - v1[0:1, :]), 0.0) / z
    w2 = jnp.where(rank2 < kk, jnp.exp(s2 - v2[0:1, :]), 0.0)
    excess = jnp.max(jnp.abs(n1 - kk) + jnp.abs(n2 - kk) + jnp.abs(nc - kk))
    return w1, cnt1, w2, rank2, excess


def _route_body(qt_ref, keys_ref, csel_ref, e1_ref, c1_ref, e2_ref, r2_ref):
    half = keys_ref.shape[3]

    def per_head(h, _):
        s1 = _dot3(keys_ref[h, 0], qt_ref[pl.ds(pl.multiple_of(h * 2 * half, half), half), :])
        s2 = _dot3(keys_ref[h, 1], qt_ref[pl.ds(pl.multiple_of(h * 2 * half + half, half), half), :])

        def emit(tie_exact):
            w1, cnt1, w2, rank2, excess = _route_head(s1, s2, csel_ref, tie_exact)
            e1_ref[h] = w1
            c1_ref[h] = cnt1
            e2_ref[h] = w2.astype(BF16)
            r2_ref[h] = rank2.astype(BF16)
            return excess

        excess = emit(False)

        @pl.when(excess > 0.0)
        def _redo():
            emit(True)

        return 0

    lax.fori_loop(0, PEER_HEADS, per_head, 0)


def _peer_route(qt, keys, tn):
    pq, n = qt.shape
    csel = _cand_select_mats()
    out = lambda dt: jax.ShapeDtypeStruct((PEER_HEADS, N_KEYS, n), dt)
    ospec = pl.BlockSpec((PEER_HEADS, N_KEYS, tn), lambda i: (0, 0, i))
    return pl.pallas_call(
        _route_body,
        grid=(n // tn,),
        in_specs=[pl.BlockSpec((pq, tn), lambda i: (0, i)),
                  pl.BlockSpec(keys.shape, lambda i: (0, 0, 0, 0)),
                  pl.BlockSpec(csel.shape, lambda i: (0, 0, 0))],
        out_specs=[ospec] * 4,
        out_shape=[out(F32), out(F32), out(BF16), out(BF16)],
        compiler_params=_cparams(1),
        name="peer_route",
    )(qt, keys, csel)


BF16_ROWS = 2 * SUBLANES


def _peer_body(h2_ref, u_ref, vt_ref, e1_ref, c1_ref, e2_ref, r2_ref, x1_ref, gf_ref, o_ref,
               acc, pre_s, ga, *, a_per_tile):
    e = pl.program_id(1)
    tn = h2_ref.shape[0]

    @pl.when(e == 0)
    def _zero():
        acc[...] = jnp.zeros(acc.shape, F32)

    mm_blocks = 4
    mrows = a_per_tile * N_KEYS // mm_blocks
    for al in range(a_per_tile):
        if al % (a_per_tile // mm_blocks) == 0:
            bi = al // (a_per_tile // mm_blocks)
            pre_s[bi * mrows:(bi + 1) * mrows, :] = lax.dot_general(
                u_ref[bi * mrows:(bi + 1) * mrows, :], h2_ref[...], _NT, preferred_element_type=F32)
        g = jnp.zeros((N_KEYS // BF16_ROWS, BF16_ROWS, tn), BF16)
        for h in range(PEER_HEADS):
            cnt = jnp.broadcast_to(c1_ref[h, al:al + 1, :], (BF16_ROWS, tn)).astype(BF16)
            w1 = jnp.broadcast_to(e1_ref[h, al:al + 1, :], (BF16_ROWS, tn)).astype(BF16)
            hit = r2_ref[h] < cnt[None]
            g = g + jnp.where(hit, e2_ref[h], jnp.zeros((), BF16)) * w1[None]
        ga[al * N_KEYS:(al + 1) * N_KEYS, :] = g.reshape(N_KEYS, tn)
    n_part = 2
    part = a_per_tile * N_KEYS // n_part
    for pi in range(n_part):
        prows = slice(pi * part, (pi + 1) * part)
        pre = pre_s[prows, :]
        act = 0.5 * pre * (1.0 + lax.erf(pre * (2.0 ** -0.5)))
        gact = ga[prows, :] * act.astype(BF16)
        acc[...] += jnp.dot(vt_ref[:, prows], gact, preferred_element_type=F32)

    @pl.when(e == pl.num_programs(1) - 1)
    def _done():
        o_ref[...] = x1_ref[...] + gf_ref[0] * acc[...].T


def _peer_dense(h2, u_bf, vt_bf, e1, c1, e2, r2, x1, gf, tn, te, tiles_per_mod):
    n, d = x1.shape
    n_e = u_bf.shape[0]
    a_per_tile = te // N_KEYS
    r = gf.shape[1]
    packed = lambda a: a.reshape(PEER_HEADS, N_KEYS // BF16_ROWS, BF16_ROWS, n)
    key_rows = lambda: pl.BlockSpec((PEER_HEADS, a_per_tile, tn), lambda i, e: (0, e, i))
    key_full = lambda: pl.BlockSpec((PEER_HEADS, N_KEYS // BF16_ROWS, BF16_ROWS, tn), lambda i, e: (0, 0, 0, i))
    return pl.pallas_call(
        functools.partial(_peer_body, a_per_tile=a_per_tile),
        grid=(n // tn, n_e // te),
        in_specs=[pl.BlockSpec((tn, d), lambda i, e: (i, 0)),
                  pl.BlockSpec((te, d), lambda i, e: (e, 0)),
                  pl.BlockSpec((d, te), lambda i, e: (0, e)),
                  key_rows(), key_rows(), key_full(), key_full(),
                  pl.BlockSpec((tn, d), lambda i, e: (i, 0)),
                  pl.BlockSpec((1, r, d), lambda i, e: (i // tiles_per_mod, 0, 0))],
        out_specs=pl.BlockSpec((tn, d), lambda i, e: (i, 0)),
        out_shape=jax.ShapeDtypeStruct((n, d), F32),
        scratch_shapes=[pltpu.VMEM((d, tn), F32), pltpu.VMEM((te, tn), F32), pltpu.VMEM((te, tn), BF16)],
        compiler_params=_cparams(2),
        name="peer_dense",
    )(h2, u_bf, vt_bf, e1, c1, packed(e2), packed(r2), x1, gf)


PEER_EXPERT_TILE = 1024
PEER_TOKEN_TILE = 512


def _layer_group(x, mods, per_token_mods, p, attend, conv_buf, s0, tm, tn_peer):
    b, t, d = x.shape
    n = b * t
    assert n % tm == 0 and n % tn_peer == 0 and (per_token_mods or (t % tm == 0 and t % tn_peer == 0))
    sh_a, sc_a, g_a, sh_f, sc_f, g_f = mods
    tiles_per_seq = t // tm if not per_token_mods else 1
    tiles_per_mod = n // tm if per_token_mods else tiles_per_seq
    xf = x.reshape(n, d)
    q, k, v, kt, vt, lin, z, ba, km = _inproj(xf, sc_a, sh_a, p["norm_attn"], p["w_in"], p["qg"], p["kg"],
                                              p["bd"], tm, tiles_per_mod, tiles_per_seq)
    o_attn = attend(q.reshape(b, t, ATTN_W), k.reshape(b, t, KV_W), v.reshape(b, t, KV_W), vt, km)
    tp = _round_up(t, GDN_CHUNK)
    pad3 = lambda a: jnp.pad(a.reshape(b, t, -1), ((0, 0), (0, tp - t), (0, 0)))
    cbuf8 = jnp.pad(conv_buf, ((0, 0), (SUBLANES - (CONV_W - 1), 0), (0, 0)))
    o_lin, s_new, conv_new = _gdn(pad3(lin), pad3(z), pad3(ba), p["conv_w"], cbuf8, s0,
                                  p["alog"], p["dtb"], p["og"], p["bd"], t)
    o_lin = o_lin[:, :t].reshape(n, LIN_W)
    x1, h2, qt = _outproj(o_attn.reshape(n, ATTN_W), o_lin, xf, g_a, sc_f, sh_f, p["norm_ffn"],
                          p["w_out"], p["wqt"], tm, tiles_per_mod)
    e1, c1, e2, r2 = _peer_route(qt, p["peer_keys"], tm)
    x2 = _peer_dense(h2, p["u_bf"], p["vt_bf"], e1, c1, e2, r2, x1, g_f, tn_peer, PEER_EXPERT_TILE,
                     n // tn_peer if per_token_mods else t // tn_peer)
    to_rows = lambda a: a.reshape(-1, N_KV_HEADS, HEAD_DIM, a.shape[-1]).transpose(0, 3, 1, 2)
    return (x2.reshape(b, t, d), to_rows(kt).reshape(b, t, N_KV_HEADS, HEAD_DIM),
            to_rows(vt).reshape(b, t, N_KV_HEADS, HEAD_DIM), s_new, conv_new)


def kernel(x_prompt, x_sample, cache_k, cache_v, state_ssm, state_conv, page_table, c_prompt, c_sample,
           rel_bias, w_ada, b_ada, norm_attn, norm_ffn, w_in, w_out, q_norm, k_norm, conv_w, a_log,
           dt_bias, o_norm, peer_wq, peer_keys, peer_u, peer_v):
    depth = w_ada.shape[0]
    b, t, d = x_prompt.shape
    b2, s, _ = x_sample.shape
    n_pool = cache_k.shape[1]
    n_pages = page_table.shape[1]
    past = n_pages * PAGE_SIZE
    n_blocks = past // MOBA_BLOCK
    s_pad = 16
    assert t % MOBA_BLOCK == 0 and past % MOBA_BLOCK == 0 and n_pages % PAGES_PER_STEP == 0
    assert MOBA_BLOCK + 1 >= MAX_DISTANCE and s <= s_pad and n_blocks <= LANES
    assert w_in.shape[2] + LANES - 2 * H_LIN == D_IN_PAD

    mod = _ada(jnp.concatenate([c_prompt, c_sample], axis=0), w_ada, b_ada)
    btab = _relbias(rel_bias)
    rb_flat = rel_bias.reshape(-1)

    pool_view = lambda c: c.transpose(0, 1, 3, 4, 2).reshape(depth * n_pool, KV_W, PAGE_SIZE)
    ckt, cvt = pool_view(cache_k), pool_view(cache_v)
    bd = (jnp.arange(ATTN_W)[:, None] // HEAD_DIM == jnp.arange(ATTN_W)[None, :] // HEAD_DIM).astype(BF16)

    rows = N_HEADS * s
    bias_last = btab[1, :, :, :s].transpose(0, 2, 1).reshape(rows, MOBA_BLOCK)
    own = btab[0, :, :s_pad, :s].transpose(0, 2, 1)
    causal = jnp.arange(s_pad)[None, :] <= jnp.arange(s)[:, None]
    bias_own = jnp.where(causal[None], own, NEG_INF).reshape(rows, s_pad)
    far = jnp.repeat(rel_bias[N_BUCKETS - 1], s).reshape(rows, 1)
    kv_of_row = jnp.arange(rows) // (s * GQA)
    lane_kv = jnp.arange(KV_W) // HEAD_DIM
    row_mask = (kv_of_row[:, None] == lane_kv[None, :]).astype(F32)
    nbp = _round_up(t // MOBA_BLOCK, SUBLANES)

    xp, xs = x_prompt, x_sample
    outs = [[] for _ in range(8)]
    for l in range(depth):
        lane_pad = lambda a: jnp.pad(a[l][None, :], ((0, 0), (H_LIN, LANES - 2 * H_LIN)))
        p = {
            "norm_attn": norm_attn[l][None, :], "norm_ffn": norm_ffn[l][None, :],
            "w_in": jnp.pad(w_in[l], ((0, 0), (0, D_IN_PAD - w_in.shape[2]))).astype(BF16),
            "w_out": w_out[l].astype(BF16),
            "qg": jnp.tile(q_norm[l], N_HEADS)[None, :], "kg": jnp.tile(k_norm[l], N_KV_HEADS)[None, :],
            "bd": bd, "conv_w": conv_w[l], "alog": lane_pad(a_log), "dtb": lane_pad(dt_bias),
            "og": o_norm[l][None, :], "wqt": peer_wq[l].T.astype(BF16), "peer_keys": peer_keys[l],
            "u_bf": peer_u[l].astype(BF16), "vt_bf": peer_v[l].T.astype(BF16),
        }
        m6 = jnp.split(mod[l], 6, axis=-1)
        mods_p = [m[:b][:, None, :] for m in m6]
        mods_s = [jnp.repeat(m[b:], s, axis=0)[None] for m in m6]

        def attend_p(q, k, v, vt, km):
            kmean = jnp.pad(km.reshape(b, t // MOBA_BLOCK, KV_W), ((0, 0), (0, nbp - t // MOBA_BLOCK), (0, 0)))
            return _moba_prompt(rb_flat, q, k, vt, kmean, btab)

        pt_flat = (page_table + l * n_pool).reshape(-1).astype(I32)

        def attend_s(q, k, v, vt, km):
            qr = q.reshape(b2, s, N_HEADS, HEAD_DIM).transpose(0, 2, 1, 3).reshape(b2, rows, HEAD_DIM)
            qw = jnp.tile(qr, (1, 1, N_KV_HEADS)) * row_mask[None]
            padn = lambda a: jnp.pad(a, ((0, 0), (0, s_pad - s), (0, 0)))
            o = _moba_sample(pt_flat, ckt, cvt, qw, padn(k), padn(v), bias_last, bias_own, far, n_pages)
            return o.reshape(b2, N_HEADS, s, HEAD_DIM).transpose(0, 2, 1, 3).reshape(b2, s, ATTN_W)

        conv0 = jnp.zeros((b, CONV_W - 1, CONV_CH), F32)
        ssm0 = jnp.zeros((b, H_LIN, HEAD_DIM, HEAD_DIM), F32)
        xp, kp, vp, sp, cp = _layer_group(xp, mods_p, False, p, attend_p, conv0, ssm0, MOBA_BLOCK, PEER_TOKEN_TILE)
        xs, ks, vs, ss, cs = _layer_group(xs, mods_s, True, p, attend_s, state_conv[l], state_ssm[l], b2 * s, b2 * s)
        for lst, val in zip(outs, (kp, vp, ks, vs, sp, ss, cp, cs)):
            lst.append(val)
    return (xp, xs) + tuple(jnp.stack(o) for o in outs)
```

```python
import functools
import math

import numpy as np
import jax
import jax.numpy as jnp
from jax import lax
from jax.experimental import pallas as pl
from jax.experimental.pallas import tpu as pltpu

F32 = jnp.float32
BF16 = jnp.bfloat16
I32 = jnp.int32

HEAD_DIM = 64
N_HEADS = 8
N_KV_HEADS = 4
GQA = N_HEADS // N_KV_HEADS
ATTN_W = N_HEADS * HEAD_DIM
KV_W = N_KV_HEADS * HEAD_DIM
MOBA_BLOCK = 256
MOBA_TOPK = 3
N_BUCKETS = 32
MAX_DISTANCE = 128
H_LIN = 8
LIN_W = H_LIN * HEAD_DIM
CONV_W = 4
CONV_CH = 3 * LIN_W
GDN_CHUNK = 64
GDN_CHUNKS_PER_STEP = 4
INV_BLOCK = 16
PEER_HEADS = 8
N_KEYS = 128
PEER_TOPK = 16
PAGE_SIZE = 128
EPS = 1e-6

LANES = 128
SUBLANES = 8
VMEM_LIMIT_BYTES = 56 * 1024 * 1024

D_IN_PAD = ATTN_W + 2 * KV_W + CONV_CH + LIN_W + LANES
NEG_INF = float("-inf")

_NT = (((1,), (1,)), ((), ()))
_TN = (((0,), (0,)), ((), ()))


def _cparams(n_axes):
    return pltpu.CompilerParams(dimension_semantics=("arbitrary",) * n_axes,
                                vmem_limit_bytes=VMEM_LIMIT_BYTES)


def _round_up(x, m):
    return -(-x // m) * m


def _dot(a, b):
    return jnp.dot(a.astype(BF16), b.astype(BF16), preferred_element_type=F32)


def _dot_nt(a, b):
    return lax.dot_general(a.astype(BF16), b.astype(BF16), _NT, preferred_element_type=F32)


def _dot_tn(a, b):
    return lax.dot_general(a.astype(BF16), b.astype(BF16), _TN, preferred_element_type=F32)


def _bmm(a, b):
    return jnp.einsum("bij,bjk->bik", a.astype(BF16), b.astype(BF16), preferred_element_type=F32)


def _bmm_nt(a, b):
    return jnp.einsum("bik,bjk->bij", a.astype(BF16), b.astype(BF16), preferred_element_type=F32)


def _split2(a):
    hi = a.astype(BF16)
    lo = (a - hi.astype(F32)).astype(BF16)
    return hi, lo


def _split3(a):
    hi = a.astype(BF16)
    r = a - hi.astype(F32)
    mid = r.astype(BF16)
    lo = (r - mid.astype(F32)).astype(BF16)
    return hi, mid, lo


def _dot3(a, b):
    ah, al = _split2(a)
    bh, bl = _split2(b)
    d = functools.partial(jnp.dot, preferred_element_type=F32)
    return d(ah, bh) + (d(al, bh) + d(ah, bl))


def _dot3_nt(a, b):
    ah, al = _split2(a)
    bh, bl = _split2(b)
    d = functools.partial(lax.dot_general, dimension_numbers=_NT, preferred_element_type=F32)
    return d(ah, bh) + (d(al, bh) + d(ah, bl))


def _dot_mask_rhs(a, mask_bf16):
    hi, mid, lo = _split3(a)
    d = functools.partial(jnp.dot, preferred_element_type=F32)
    return d(hi, mask_bf16) + (d(mid, mask_bf16) + d(lo, mask_bf16))


def _dot_mask_lhs(mask_bf16, b):
    hi, mid, lo = _split3(b)
    d = functools.partial(jnp.dot, preferred_element_type=F32)
    return d(mask_bf16, hi) + (d(mask_bf16, mid) + d(mask_bf16, lo))


def _sigmoid(x):
    return 1.0 / (1.0 + jnp.exp(-x))


def _softplus(x):
    return jnp.maximum(x, 0.0) + jnp.log(1.0 + jnp.exp(-jnp.abs(x)))


def _head_sumsq(a, bd):
    return _dot_mask_rhs(a * a, bd)


def _rank_desc(g, idx, n, axis):
    rank = jnp.zeros(g.shape, F32)
    for m in range(n):
        gm = lax.slice_in_dim(g, m, m + 1, axis=axis)
        beats = (gm > g) | ((gm == g) & (idx > m))
        rank = rank + jnp.where(beats, 1.0, 0.0)
    return rank


def _ada_body(c_ref, w_ref, b_ref, o_ref):
    c = c_ref[...]
    s = c * _sigmoid(c)
    o_ref[0] = _dot3(s, w_ref[0]) + b_ref[0]


def _ada(c, w_ada, b_ada):
    n_l, d, d6 = w_ada.shape
    r = c.shape[0]
    tn = 512
    return pl.pallas_call(
        _ada_body,
        grid=(n_l, d6 // tn),
        in_specs=[pl.BlockSpec((r, d), lambda l, j: (0, 0)),
                  pl.BlockSpec((1, d, tn), lambda l, j: (l, 0, j)),
                  pl.BlockSpec((1, 1, tn), lambda l, j: (l, 0, j))],
        out_specs=pl.BlockSpec((1, r, tn), lambda l, j: (l, 0, j)),
        out_shape=jax.ShapeDtypeStruct((n_l, r, d6), F32),
        compiler_params=_cparams(2),
        name="ada",
    )(c, w_ada, b_ada.reshape(n_l, 1, d6))


def _inproj_body(x_ref, sc_ref, sh_ref, gn_ref, w_ref, qg_ref, kg_ref, bd_ref,
                 q_ref, k_ref, v_ref, kt_ref, vt_ref, lin_ref, z_ref, ba_ref, km_ref):
    x = x_ref[...]
    ms = jnp.mean(x * x, axis=-1, keepdims=True)
    h = x * lax.rsqrt(ms + EPS) * gn_ref[...]
    h = h * (1.0 + sc_ref[0]) + sh_ref[0]
    proj = jnp.dot(h.astype(BF16), w_ref[...], preferred_element_type=F32)
    bd = bd_ref[...]
    aq = proj[:, :ATTN_W]
    ak = proj[:, ATTN_W:ATTN_W + KV_W]
    q = aq * lax.rsqrt(_head_sumsq(aq, bd) * (1.0 / HEAD_DIM) + EPS) * qg_ref[...]
    k = ak * lax.rsqrt(_head_sumsq(ak, bd[:KV_W, :KV_W]) * (1.0 / HEAD_DIM) + EPS) * kg_ref[...]
    q_ref[...] = q * (HEAD_DIM ** -0.5)
    k_ref[...] = k
    kt_ref[0] = k.T
    o = ATTN_W + KV_W
    v = proj[:, o:o + KV_W]
    v_ref[...] = v
    vt_ref[0] = v.T
    o += KV_W
    lin_ref[...] = proj[:, o:o + CONV_CH]
    o += CONV_CH
    z_ref[...] = proj[:, o:o + LIN_W]
    o += LIN_W
    ba_ref[...] = proj[:, o:o + LANES]
    km_ref[0] = jnp.mean(k, axis=0, keepdims=True)


def _inproj(x, sc, sh, gn, w_pad, qg, kg, bd, tm, tiles_per_mod, tiles_per_seq):
    n, d = x.shape
    r = sc.shape[1]
    nt = n // tm
    n_seq = nt // tiles_per_seq
    mod_spec = pl.BlockSpec((1, r, d), lambda i: (i // tiles_per_mod, 0, 0))
    const = lambda shape: pl.BlockSpec(shape, lambda i: (0,) * len(shape))
    row = lambda w: pl.BlockSpec((tm, w), lambda i: (i, 0))
    tspec = pl.BlockSpec((1, KV_W, tm), lambda i: (i // tiles_per_seq, 0, i % tiles_per_seq))
    tshape = jax.ShapeDtypeStruct((n_seq, KV_W, tiles_per_seq * tm), F32)
    return pl.pallas_call(
        _inproj_body,
        grid=(nt,),
        in_specs=[row(d), mod_spec, mod_spec, const((1, d)), const(w_pad.shape),
                  const((1, ATTN_W)), const((1, KV_W)), const((ATTN_W, ATTN_W))],
        out_specs=[row(ATTN_W), row(KV_W), row(KV_W), tspec, tspec, row(CONV_CH), row(LIN_W), row(LANES),
                   pl.BlockSpec((1, 1, KV_W), lambda i: (i, 0, 0))],
        out_shape=[jax.ShapeDtypeStruct((n, ATTN_W), F32), jax.ShapeDtypeStruct((n, KV_W), F32),
                   jax.ShapeDtypeStruct((n, KV_W), F32), tshape, tshape,
                   jax.ShapeDtypeStruct((n, CONV_CH), F32),
                   jax.ShapeDtypeStruct((n, LIN_W), F32), jax.ShapeDtypeStruct((n, LANES), F32),
                   jax.ShapeDtypeStruct((nt, 1, KV_W), F32)],
        compiler_params=_cparams(1),
        name="inproj",
    )(x, sc, sh, gn, w_pad, qg, kg, bd)


def _relbias_body(rb_ref, o_ref):
    n_tab, n_h, r, c = o_ref.shape
    key = lax.broadcasted_iota(I32, (r, c), 0)
    qry = lax.broadcasted_iota(I32, (r, c), 1)
    max_exact = N_BUCKETS // 2
    for t in range(n_tab):
        dist = jnp.maximum(qry - key + t * MOBA_BLOCK, 0)
        nf = jnp.maximum(dist, 1).astype(F32)
        large = max_exact + (jnp.log(nf / max_exact) / math.log(MAX_DISTANCE / max_exact)
                             * (N_BUCKETS - max_exact)).astype(I32)
        bucket = jnp.where(dist < max_exact, dist, jnp.minimum(large, N_BUCKETS - 1))

        def per_head(h, _, bucket=bucket, t=t):
            acc = jnp.zeros((r, c), F32)
            for j in range(N_BUCKETS):
                acc = jnp.where(bucket == j, rb_ref[j * n_h + h], acc)
            o_ref[t, h] = acc
            return 0

        lax.fori_loop(0, n_h, per_head, 0)


def _relbias(rel_bias):
    n_h = rel_bias.shape[1]
    return pl.pallas_call(
        _relbias_body,
        in_specs=[pl.BlockSpec(memory_space=pltpu.SMEM)],
        out_specs=pl.BlockSpec(memory_space=pltpu.VMEM),
        out_shape=jax.ShapeDtypeStruct((2, n_h, MOBA_BLOCK, MOBA_BLOCK), F32),
        compiler_params=pltpu.CompilerParams(vmem_limit_bytes=VMEM_LIMIT_BYTES),
        name="relbias",
    )(rel_bias.reshape(-1))


def _moba_prompt_body(sch_ref, rb_ref, q_ref, km_ref, k_ref, vt_ref, bown_ref, bprev_ref, o_ref,
                      selt, m_s, l_s, acct):
    pair = pl.program_id(1)
    i = sch_ref[pair]
    n = sch_ref[pl.num_programs(1) + pair]
    nbp = km_ref.shape[1]
    blk = q_ref.shape[1]
    far_bucket = N_BUCKETS - 1

    def hs(h):
        return slice(h * HEAD_DIM, (h + 1) * HEAD_DIM)

    def group_q(kv):
        return jnp.concatenate([q_ref[0, :, hs(kv * GQA + g)] for g in range(GQA)], axis=0)

    def group_lanes(fn, kv):
        return jnp.concatenate([fn(kv * GQA + g) for g in range(GQA)], axis=1)

    def scores(kv):
        return _dot_nt(k_ref[0, :, hs(kv)], group_q(kv))

    def vtk(kv):
        return vt_ref[0, kv * HEAD_DIM:(kv + 1) * HEAD_DIM, :]

    @pl.when(n == 0)
    def _first():
        row = lax.broadcasted_iota(I32, (nbp, GQA * blk), 0)
        kr = lax.broadcasted_iota(I32, (blk, blk), 0)
        qc = lax.broadcasted_iota(I32, (blk, blk), 1)
        causal = jnp.where(kr <= qc, 0.0, NEG_INF)
        for kv in range(N_KV_HEADS):
            gate = _dot3_nt(km_ref[0, :, hs(kv)], group_q(kv))
            g = jnp.where(row < i, gate, NEG_INF)
            rank = _rank_desc(g, row, nbp, 0)
            selt[kv] = jnp.where((rank < MOBA_TOPK) & (row < i), 1.0, 0.0)
            s = scores(kv) + group_lanes(lambda h: bown_ref[h] + causal, kv)
            m = jnp.max(s, axis=0, keepdims=True)
            p = jnp.exp(s - m)
            m_s[kv] = m
            l_s[kv] = jnp.sum(p, axis=0, keepdims=True)
            acct[kv] = _dot(vtk(kv), p)

    @pl.when(n >= 1)
    def _past():
        kb = n - 1
        is_prev = kb == i - 1
        for kv in range(N_KV_HEADS):
            far = group_lanes(lambda h: jnp.full((1, blk), rb_ref[far_bucket * N_HEADS + h], F32), kv)
            bias = jnp.where(is_prev, group_lanes(lambda h: bprev_ref[h], kv), far)
            sel = selt[kv, pl.ds(kb, 1), :]
            s = jnp.where(sel > 0.5, scores(kv) + bias, NEG_INF)
            m_old = m_s[kv]
            m_new = jnp.maximum(m_old, jnp.max(s, axis=0, keepdims=True))
            alpha = jnp.exp(m_old - m_new)
            p = jnp.exp(s - m_new)
            m_s[kv] = m_new
            l_s[kv] = alpha * l_s[kv] + jnp.sum(p, axis=0, keepdims=True)
            acct[kv] = alpha * acct[kv] + _dot(vtk(kv), p)

    @pl.when(n == i)
    def _done():
        outs = []
        for kv in range(N_KV_HEADS):
            a = acct[kv] / l_s[kv]
            outs += [a[:, g * blk:(g + 1) * blk] for g in range(GQA)]
        o_ref[0] = jnp.concatenate(outs, axis=0).T


def _moba_prompt(rb_flat, q, k, vt, kmean, btab):
    b, t, _ = q.shape
    nb = t // MOBA_BLOCK
    nbp = kmean.shape[1]
    blk = MOBA_BLOCK

    pairs = [(i, n) for i in range(nb) for n in range(i + 1)]
    n_pairs = len(pairs)
    sched = jnp.asarray([p[0] for p in pairs] + [p[1] for p in pairs], I32)

    def q_blk(pr, sch):
        return sch[pr]

    def kv_blk(pr, sch):
        i, n = sch[pr], sch[n_pairs + pr]
        return jnp.where(n == 0, i, n - 1)

    return pl.pallas_call(
        _moba_prompt_body,
        grid_spec=pltpu.PrefetchScalarGridSpec(
            num_scalar_prefetch=1,
            grid=(b, n_pairs),
            in_specs=[pl.BlockSpec(memory_space=pltpu.SMEM),
                      pl.BlockSpec((1, blk, ATTN_W), lambda bi, pr, sch: (bi, q_blk(pr, sch), 0)),
                      pl.BlockSpec((1, nbp, KV_W), lambda bi, pr, sch: (bi, 0, 0)),
                      pl.BlockSpec((1, blk, KV_W), lambda bi, pr, sch: (bi, kv_blk(pr, sch), 0)),
                      pl.BlockSpec((1, KV_W, blk), lambda bi, pr, sch: (bi, 0, kv_blk(pr, sch))),
                      pl.BlockSpec((None, N_HEADS, blk, blk), lambda bi, pr, sch: (0, 0, 0, 0)),
                      pl.BlockSpec((None, N_HEADS, blk, blk), lambda bi, pr, sch: (1, 0, 0, 0))],
            out_specs=pl.BlockSpec((1, blk, ATTN_W), lambda bi, pr, sch: (bi, q_blk(pr, sch), 0)),
            scratch_shapes=[pltpu.VMEM((N_KV_HEADS, nbp, GQA * blk), F32),
                            pltpu.VMEM((N_KV_HEADS, 1, GQA * blk), F32),
                            pltpu.VMEM((N_KV_HEADS, 1, GQA * blk), F32),
                            pltpu.VMEM((N_KV_HEADS, HEAD_DIM, GQA * blk), F32)]),
        out_shape=jax.ShapeDtypeStruct((b, t, ATTN_W), F32),
        compiler_params=_cparams(2),
        name="moba_prompt",
    )(sched, rb_flat, q, kmean, k, vt, btab, btab)


PAGES_PER_STEP = 32
PAGES_PER_BLOCK = MOBA_BLOCK // PAGE_SIZE
BLOCKS_PER_STEP = PAGES_PER_STEP // PAGES_PER_BLOCK


def _moba_sample_body(pt_ref, qw_ref, kn_ref, vn_ref, bl_ref, bo_ref, far_ref, *refs, n_blocks):
    kp = refs[:PAGES_PER_STEP]
    vp = refs[PAGES_PER_STEP:2 * PAGES_PER_STEP]
    o_ref, kres, kmt, selc, m_s, l_s, acc = refs[2 * PAGES_PER_STEP:]
    ph = pl.program_id(1)
    j = pl.program_id(2)
    n_steps = pl.num_programs(2)
    rows = qw_ref.shape[1]
    qw = qw_ref[0]
    qw_b = qw.astype(BF16)

    @pl.when(ph == 0)
    def _scan_keys():
        lane = lax.broadcasted_iota(I32, (KV_W, LANES), 1)

        @pl.when(j == 0)
        def _zero():
            kmt[...] = jnp.zeros(kmt.shape, F32)

        cur = kmt[...]
        for nn in range(BLOCKS_PER_STEP):
            tot = kp[nn * PAGES_PER_BLOCK][0]
            for p in range(1, PAGES_PER_BLOCK):
                tot = tot + kp[nn * PAGES_PER_BLOCK + p][0]
            col = jnp.sum(tot, axis=1, keepdims=True) * (1.0 / MOBA_BLOCK)
            cur = jnp.where(lane == j * BLOCKS_PER_STEP + nn, col, cur)
            for p in range(PAGES_PER_BLOCK):
                r = nn * PAGES_PER_BLOCK + p
                kres[j * PAGES_PER_STEP + r] = kp[r][0].astype(BF16)
        kmt[...] = cur

    @pl.when((ph == 1) & (j == 0))
    def _select():
        gate = _dot3(qw, kmt[...])
        col = lax.broadcasted_iota(I32, (rows, LANES), 1)
        g = jnp.where(col < n_blocks, gate, NEG_INF)
        rank = _rank_desc(g, col, n_blocks, 1)
        sel = jnp.where((rank < MOBA_TOPK) & (col < n_blocks), 1.0, 0.0)
        for jj in range(n_blocks // BLOCKS_PER_STEP):
            selc[jj] = sel[:, jj * BLOCKS_PER_STEP:(jj + 1) * BLOCKS_PER_STEP]
        s = _dot_nt(qw_b, kn_ref[0]) + bo_ref[...]
        m = jnp.max(s, axis=-1, keepdims=True)
        p = jnp.exp(s - m)
        m_s[...] = m
        l_s[...] = jnp.sum(p, axis=-1, keepdims=True)
        acc[...] = _dot(p, vn_ref[0])

    @pl.when(ph == 1)
    def _attend():
        sel_j = selc[j]
        pieces = []
        for nn in range(BLOCKS_PER_STEP):
            s = jnp.concatenate(
                [jnp.dot(qw_b, kres[j * PAGES_PER_STEP + nn * PAGES_PER_BLOCK + p],
                         preferred_element_type=F32) for p in range(PAGES_PER_BLOCK)], axis=1)
            if nn == BLOCKS_PER_STEP - 1:
                s = s + jnp.where(j == n_steps - 1, bl_ref[...], far_ref[...])
            else:
                s = s + far_ref[...]
            pieces.append(jnp.where(sel_j[:, nn:nn + 1] > 0.5, s, NEG_INF))
        s_all = jnp.concatenate(pieces, axis=1)
        m_old = m_s[...]
        m_new = jnp.maximum(m_old, jnp.max(s_all, axis=-1, keepdims=True))
        alpha = jnp.exp(m_old - m_new)
        p_all = jnp.exp(s_all - m_new)
        m_s[...] = m_new
        l_s[...] = alpha * l_s[...] + jnp.sum(p_all, axis=-1, keepdims=True)
        p_b = p_all.astype(BF16)
        pv = jnp.zeros(acc.shape, F32)
        for r in range(PAGES_PER_STEP):
            pv = pv + lax.dot_general(p_b[:, r * PAGE_SIZE:(r + 1) * PAGE_SIZE], vp[r][0].astype(BF16),
                                      _NT, preferred_element_type=F32)
        acc[...] = alpha * acc[...] + pv

    @pl.when((ph == 1) & (j == n_steps - 1))
    def _done():
        out = acc[...] / l_s[...]
        rpk = rows // N_KV_HEADS
        for kv in range(N_KV_HEADS):
            o_ref[0, kv * rpk:(kv + 1) * rpk, :] = out[kv * rpk:(kv + 1) * rpk,
                                                      kv * HEAD_DIM:(kv + 1) * HEAD_DIM]


def _moba_sample(pt_flat, ckt, cvt, qw, kn, vn, bias_last, bias_own, far, n_pages):
    b2, rows, _ = qw.shape
    n_blocks = n_pages // PAGES_PER_BLOCK
    n_steps = n_pages // PAGES_PER_STEP
    s_pad = kn.shape[1]
    last = n_steps - 1

    def k_spec(r):
        return pl.BlockSpec((1, KV_W, PAGE_SIZE), lambda b, ph, j, pt: (
            pt[b * n_pages + jnp.where(ph == 0, j, last) * PAGES_PER_STEP + r], 0, 0))

    def v_spec(r):
        return pl.BlockSpec((1, KV_W, PAGE_SIZE), lambda b, ph, j, pt: (
            pt[b * n_pages + jnp.where(ph == 0, 0, j) * PAGES_PER_STEP + r], 0, 0))

    per_b = lambda shape: pl.BlockSpec((1,) + shape, lambda b, ph, j, pt: (b, 0, 0))
    const = lambda shape: pl.BlockSpec(shape, lambda b, ph, j, pt: (0, 0))
    return pl.pallas_call(
        functools.partial(_moba_sample_body, n_blocks=n_blocks),
        grid_spec=pltpu.PrefetchScalarGridSpec(
            num_scalar_prefetch=1,
            grid=(b2, 2, n_steps),
            in_specs=[per_b((rows, KV_W)), per_b((s_pad, KV_W)), per_b((s_pad, KV_W)),
                      const((rows, MOBA_BLOCK)), const((rows, s_pad)), const((rows, 1))]
            + [k_spec(r) for r in range(PAGES_PER_STEP)] + [v_spec(r) for r in range(PAGES_PER_STEP)],
            out_specs=pl.BlockSpec((1, rows, HEAD_DIM), lambda b, ph, j, pt: (b, 0, 0)),
            scratch_shapes=[pltpu.VMEM((n_pages, KV_W, PAGE_SIZE), BF16),
                            pltpu.VMEM((KV_W, LANES), F32),
                            pltpu.VMEM((n_steps, rows, BLOCKS_PER_STEP), F32),
                            pltpu.VMEM((rows, 1), F32),
                            pltpu.VMEM((rows, 1), F32),
                            pltpu.VMEM((rows, KV_W), F32)]),
        out_shape=jax.ShapeDtypeStruct((b2, rows, HEAD_DIM), F32),
        compiler_params=_cparams(3),
        name="moba_sample",
    )(pt_flat, qw, kn, vn, bias_last, bias_own, far,
      *([ckt] * PAGES_PER_STEP), *([cvt] * PAGES_PER_STEP))


def _gdn_body(lin_ref, z_ref, ba_ref, cw_ref, cb_ref, s0_ref, al_ref, dtb_ref, og_ref, bd_ref,
              o_ref, sout_ref, cout_ref, xbuf, state, *, t_valid, nc):
    t = pl.program_id(1)
    n_t = pl.num_programs(1)
    c = GDN_CHUNK
    tt = nc * c
    halo = SUBLANES

    @pl.when(t == 0)
    def _init():
        xbuf[0:halo, :] = cb_ref[0]
        state[...] = s0_ref[0]

    @pl.when(t > 0)
    def _carry():
        xbuf[0:halo, :] = xbuf[tt:tt + halo, :]

    xbuf[halo:halo + tt, :] = lin_ref[0]
    first = halo - (CONV_W - 1)
    conv = cw_ref[0:1, :] * xbuf[pl.ds(first, tt), :]
    for i in range(1, CONV_W):
        conv = conv + cw_ref[i:i + 1, :] * xbuf[pl.ds(first + i, tt), :]
    a = conv * _sigmoid(conv)
    lq = a[:, :LIN_W]
    lk = a[:, LIN_W:2 * LIN_W]
    lv = a[:, 2 * LIN_W:]
    bd = bd_ref[...]
    lq = lq * lax.rsqrt(_head_sumsq(lq, bd) + EPS) * (HEAD_DIM ** -0.5)
    lk = lk * lax.rsqrt(_head_sumsq(lk, bd) + EPS)
    ba = ba_ref[0]
    beta_all = _sigmoid(ba)
    g_all = -jnp.exp(al_ref[...]) * _softplus(ba + dtb_ref[...])
    if t_valid % tt != 0:
        rid = t * tt + lax.broadcasted_iota(I32, (tt, 1), 0)
        ok = rid < t_valid
        lq = jnp.where(ok, lq, 0.0)
        lk = jnp.where(ok, lk, 0.0)
        lv = jnp.where(ok, lv, 0.0)
        beta_all = jnp.where(ok, beta_all, 0.0)
        g_all = jnp.where(ok, g_all, 0.0)

    it = lax.broadcasted_iota(I32, (tt, tt), 0)
    jt = lax.broadcasted_iota(I32, (tt, tt), 1)
    tril = jnp.where((it >= jt) & (it // c == jt // c), 1.0, 0.0).astype(BF16)
    gc_all = _dot_mask_lhs(tril, g_all)

    def heads(x):
        return jnp.stack([x[ci * c:(ci + 1) * c, h * HEAD_DIM:(h + 1) * HEAD_DIM]
                          for ci in range(nc) for h in range(H_LIN)], axis=0)

    def cols(x, off):
        return jnp.stack([x[ci * c:(ci + 1) * c, off + h:off + h + 1]
                          for ci in range(nc) for h in range(H_LIN)], axis=0)

    q, k, v = heads(lq), heads(lk), heads(lv)
    beta = cols(beta_all, 0)
    gc = cols(gc_all, H_LIN)
    ii = lax.broadcasted_iota(I32, (1, c, c), 1)
    jj = lax.broadcasted_iota(I32, (1, c, c), 2)
    lower = ii >= jj
    strict = ii > jj
    diag_blk = (ii // INV_BLOCK) == (jj // INV_BLOCK)
    gc_row = jnp.sum(jnp.where(ii == jj, gc, 0.0), axis=1, keepdims=True)
    eg = jnp.exp(gc)
    dmat = jnp.where(lower, jnp.exp(jnp.minimum(gc - gc_row, 0.0)), 0.0)
    kb = k * beta
    vb = v * beta
    lm = _bmm_nt(kb, k) * jnp.where(strict, dmat, 0.0)
    dm = jnp.where(diag_blk, lm, 0.0)
    nm = lm - dm
    xp = -dm
    p = _bmm(dm, dm)
    steps = int(math.log2(INV_BLOCK)) - 1
    for s in range(steps):
        xp = xp + p + _bmm(xp, p)
        if s + 1 < steps:
            p = _bmm(p, p)
    mm = nm + _bmm(xp, nm)
    yp = -mm
    pm = _bmm(mm, mm)
    msteps = int(math.log2(c // INV_BLOCK)) - 1
    for s in range(msteps):
        yp = yp + pm + _bmm(yp, pm)
        if s + 1 < msteps:
            pm = _bmm(pm, pm)
    tp = yp + xp + _bmm(yp, xp)
    kbg = kb * eg
    u = vb + _bmm(tp, vb)
    w = kbg + _bmm(tp, kbg)
    qk = _bmm_nt(q, k) * dmat
    qd = q * eg
    gl = gc[:, c - 1:c, :]
    kt = k * jnp.exp(gl - gc)
    egl = jnp.exp(gl)

    s_cur = state[...]
    for ci in range(nc):
        sl = slice(ci * H_LIN, (ci + 1) * H_LIN)
        v_new = u[sl] - _bmm(w[sl], s_cur)
        o = _bmm(qd[sl], s_cur) + _bmm(qk[sl], v_new)
        upd = jnp.stack([_dot_tn(kt[ci * H_LIN + h], v_new[h]) for h in range(H_LIN)], axis=0)
        s_cur = s_cur * egl[sl] + upd
        on = o * lax.rsqrt(jnp.mean(o * o, axis=-1, keepdims=True) + EPS) * og_ref[...]
        for h in range(H_LIN):
            zh = z_ref[0, ci * c:(ci + 1) * c, h * HEAD_DIM:(h + 1) * HEAD_DIM]
            o_ref[0, ci * c:(ci + 1) * c, h * HEAD_DIM:(h + 1) * HEAD_DIM] = on[h] * (zh * _sigmoid(zh))
    state[...] = s_cur

    @pl.when(t == n_t - 1)
    def _fin():
        sout_ref[0] = state[...]
        tv = t_valid - (t_valid - 1) // tt * tt
        cout_ref[0] = xbuf[pl.ds(halo + tv - (CONV_W - 1), CONV_W - 1), :]


def _gdn(lin, z, ba, conv_w, cbuf8, s0, alog_l, dtb_l, og, bd, t_valid):
    b, tp, _ = lin.shape
    c = GDN_CHUNK
    nc = GDN_CHUNKS_PER_STEP if tp % (GDN_CHUNKS_PER_STEP * c) == 0 else 1
    tt = nc * c
    n_t = tp // tt
    tile = lambda w: pl.BlockSpec((1, tt, w), lambda bi, t: (bi, t, 0))
    const = lambda shape: pl.BlockSpec(shape, lambda bi, t: (0,) * len(shape))
    return pl.pallas_call(
        functools.partial(_gdn_body, t_valid=t_valid, nc=nc),
        grid=(b, n_t),
        in_specs=[tile(CONV_CH), tile(LIN_W), tile(LANES), const((CONV_W, CONV_CH)),
                  pl.BlockSpec((1, SUBLANES, CONV_CH), lambda bi, t: (bi, 0, 0)),
                  pl.BlockSpec((1, H_LIN, HEAD_DIM, HEAD_DIM), lambda bi, t: (bi, 0, 0, 0)),
                  const((1, LANES)), const((1, LANES)), const((1, HEAD_DIM)), const((LIN_W, LIN_W))],
        out_specs=[tile(LIN_W),
                   pl.BlockSpec((1, H_LIN, HEAD_DIM, HEAD_DIM), lambda bi, t: (bi, 0, 0, 0)),
                   pl.BlockSpec((1, CONV_W - 1, CONV_CH), lambda bi, t: (bi, 0, 0))],
        out_shape=[jax.ShapeDtypeStruct((b, tp, LIN_W), F32),
                   jax.ShapeDtypeStruct((b, H_LIN, HEAD_DIM, HEAD_DIM), F32),
                   jax.ShapeDtypeStruct((b, CONV_W - 1, CONV_CH), F32)],
        scratch_shapes=[pltpu.VMEM((tt + 2 * SUBLANES, CONV_CH), F32),
                        pltpu.VMEM((H_LIN, HEAD_DIM, HEAD_DIM), F32)],
        compiler_params=_cparams(2),
        name="gdn",
    )(lin, z, ba, conv_w, cbuf8, s0, alog_l, dtb_l, og, bd)


def _outproj_body(oa_ref, ol_ref, x_ref, ga_ref, sc_ref, sh_ref, gn_ref, wo_ref, wqt_ref,
                  x1_ref, h2_ref, qt_ref):
    y = (jnp.dot(oa_ref[...].astype(BF16), wo_ref[:ATTN_W, :], preferred_element_type=F32)
         + jnp.dot(ol_ref[...].astype(BF16), wo_ref[ATTN_W:, :], preferred_element_type=F32))
    x1 = x_ref[...] + ga_ref[0] * y
    ms = jnp.mean(x1 * x1, axis=-1, keepdims=True)
    h2 = x1 * lax.rsqrt(ms + EPS) * gn_ref[...]
    h2 = (h2 * (1.0 + sc_ref[0]) + sh_ref[0]).astype(BF16)
    x1_ref[...] = x1
    h2_ref[...] = h2
    qt_ref[...] = lax.dot_general(wqt_ref[...], h2, _NT, preferred_element_type=F32)


def _outproj(oa, ol, x, ga, sc, sh, gn, wo, wqt, tm, tiles_per_mod):
    n, d = x.shape
    r = sc.shape[1]
    pq = wqt.shape[0]
    mod_spec = pl.BlockSpec((1, r, d), lambda i: (i // tiles_per_mod, 0, 0))
    const = lambda shape: pl.BlockSpec(shape, lambda i: (0,) * len(shape))
    row = lambda w: pl.BlockSpec((tm, w), lambda i: (i, 0))
    return pl.pallas_call(
        _outproj_body,
        grid=(n // tm,),
        in_specs=[row(ATTN_W), row(LIN_W), row(d), mod_spec, mod_spec, mod_spec, const((1, d)),
                  const(wo.shape), const(wqt.shape)],
        out_specs=[row(d), row(d), pl.BlockSpec((pq, tm), lambda i: (0, i))],
        out_shape=[jax.ShapeDtypeStruct((n, d), F32), jax.ShapeDtypeStruct((n, d), BF16),
                   jax.ShapeDtypeStruct((pq, n), F32)],
        compiler_params=_cparams(1),
        name="outproj",
    )(oa, ol, x, ga, sc, sh, gn, wo, wqt)


_CAND = [(r1, r2) for r1 in range(PEER_TOPK) for r2 in range(PEER_TOPK) if (r1 + 1) * (r2 + 1) <= PEER_TOPK]
_N_CAND_PAD = _round_up(len(_CAND), 16)


def _cand_select_mats():
    m = np.zeros((2, _N_CAND_PAD, PEER_TOPK), np.float32)
    for p, (r1, r2) in enumerate(_CAND):
        m[0, p, r1] = 1.0
        m[1, p, r2] = 1.0
    return jnp.asarray(m, dtype=BF16)


def _topk_rows(s, k, tie_exact):
    rows, t = s.shape
    kio = lax.broadcasted_iota(I32, (k, t), 0)
    cur = s
    rank = jnp.full((rows, t), float(k), F32)
    vals = jnp.zeros((k, t), F32)
    if tie_exact:
        iota = lax.broadcasted_iota(I32, (rows, t), 0).astype(F32)
    for j in range(k):
        m = jnp.max(cur, axis=0, keepdims=True)
        hit = cur == m
        if tie_exact:
            idx = jnp.min(jnp.where(hit, iota, float(rows)), axis=0, keepdims=True)
            hit = iota == idx
        rank = jnp.where(hit, float(j), rank)
        cur = jnp.where(hit, NEG_INF, cur)
        vals = jnp.where(kio == j, m, vals)
    removed = jnp.sum(jnp.where(rank < k, 1.0, 0.0), axis=0, keepdims=True)
    return vals, rank, removed


def _route_head(s1, s2, csel_ref, tie_exact):
    kk = PEER_TOPK
    tn = s1.shape[1]
    crow = lax.broadcasted_iota(I32, (_N_CAND_PAD, tn), 0)
    v1, rank1, n1 = _topk_rows(s1, kk, tie_exact)
    v2, rank2, n2 = _topk_rows(s2, kk, tie_exact)
    cand = _dot_mask_lhs(csel_ref[0], v1) + _dot_mask_lhs(csel_ref[1], v2)
    cand = jnp.where(crow < len(_CAND), cand, NEG_INF)
    cv, crank, nc = _topk_rows(cand, kk, tie_exact)
    per_rank = _dot_tn(csel_ref[0], jnp.where(crank < kk, 1.0, 0.0))
    cnt1 = jnp.zeros(s1.shape, F32)
    for j in range(kk):
        cnt1 = jnp.where(rank1 == float(j), per_rank[j:j + 1, :], cnt1)
    z = jnp.sum(jnp.exp(cv - cv[0:1, :]), axis=0, keepdims=True)
    w1 = jnp.where(rank1 < kk, jnp.exp(s1 - v1[0:1, :]), 0.0) / z
    w2 = jnp.where(rank2 < kk, jnp.exp(s2 - v2[0:1, :]), 0.0)
    excess = jnp.max(jnp.abs(n1 - kk) + jnp.abs(n2 - kk) + jnp.abs(nc - kk))
    return w1, cnt1, w2, rank2, excess


def _route_body(qt_ref, keys_ref, csel_ref, e1_ref, c1_ref, e2_ref, r2_ref):
    half = keys_ref.shape[3]

    def per_head(h, _):
        s1 = _dot3(keys_ref[h, 0], qt_ref[pl.ds(pl.multiple_of(h * 2 * half, half), half), :])
        s2 = _dot3(keys_ref[h, 1], qt_ref[pl.ds(pl.multiple_of(h * 2 * half + half, half), half), :])

        def emit(tie_exact):
            w1, cnt1, w2, rank2, excess = _route_head(s1, s2, csel_ref, tie_exact)
            e1_ref[h] = w1
            c1_ref[h] = cnt1
            e2_ref[h] = w2.astype(BF16)
            r2_ref[h] = rank2.astype(BF16)
            return excess

        excess = emit(False)

        @pl.when(excess > 0.0)
        def _redo():
            emit(True)

        return 0

    lax.fori_loop(0, PEER_HEADS, per_head, 0)


def _peer_route(qt, keys, tn):
    pq, n = qt.shape
    csel = _cand_select_mats()
    out = lambda dt: jax.ShapeDtypeStruct((PEER_HEADS, N_KEYS, n), dt)
    ospec = pl.BlockSpec((PEER_HEADS, N_KEYS, tn), lambda i: (0, 0, i))
    return pl.pallas_call(
        _route_body,
        grid=(n // tn,),
        in_specs=[pl.BlockSpec((pq, tn), lambda i: (0, i)),
                  pl.BlockSpec(keys.shape, lambda i: (0, 0, 0, 0)),
                  pl.BlockSpec(csel.shape, lambda i: (0, 0, 0))],
        out_specs=[ospec] * 4,
        out_shape=[out(F32), out(F32), out(BF16), out(BF16)],
        compiler_params=_cparams(1),
        name="peer_route",
    )(qt, keys, csel)


BF16_ROWS = 2 * SUBLANES


def _peer_body(h2_ref, u_ref, vt_ref, e1_ref, c1_ref, e2_ref, r2_ref, x1_ref, gf_ref, o_ref,
               acc, pre_s, ga, *, a_per_tile):
    e = pl.program_id(1)
    tn = h2_ref.shape[0]

    @pl.when(e == 0)
    def _zero():
        acc[...] = jnp.zeros(acc.shape, F32)

    mm_blocks = 4
    mrows = a_per_tile * N_KEYS // mm_blocks
    for al in range(a_per_tile):
        if al % (a_per_tile // mm_blocks) == 0:
            bi = al // (a_per_tile // mm_blocks)
            pre_s[bi * mrows:(bi + 1) * mrows, :] = lax.dot_general(
                u_ref[bi * mrows:(bi + 1) * mrows, :], h2_ref[...], _NT, preferred_element_type=F32)
        g = jnp.zeros((N_KEYS // BF16_ROWS, BF16_ROWS, tn), BF16)
        for h in range(PEER_HEADS):
            cnt = jnp.broadcast_to(c1_ref[h, al:al + 1, :], (BF16_ROWS, tn)).astype(BF16)
            w1 = jnp.broadcast_to(e1_ref[h, al:al + 1, :], (BF16_ROWS, tn)).astype(BF16)
            hit = r2_ref[h] < cnt[None]
            g = g + jnp.where(hit, e2_ref[h], jnp.zeros((), BF16)) * w1[None]
        ga[al * N_KEYS:(al + 1) * N_KEYS, :] = g.reshape(N_KEYS, tn)
    n_part = 2
    part = a_per_tile * N_KEYS // n_part
    for pi in range(n_part):
        prows = slice(pi * part, (pi + 1) * part)
        pre = pre_s[prows, :]
        act = 0.5 * pre * (1.0 + lax.erf(pre * (2.0 ** -0.5)))
        gact = ga[prows, :] * act.astype(BF16)
        acc[...] += jnp.dot(vt_ref[:, prows], gact, preferred_element_type=F32)

    @pl.when(e == pl.num_programs(1) - 1)
    def _done():
        o_ref[...] = x1_ref[...] + gf_ref[0] * acc[...].T


def _peer_dense(h2, u_bf, vt_bf, e1, c1, e2, r2, x1, gf, tn, te, tiles_per_mod):
    n, d = x1.shape
    n_e = u_bf.shape[0]
    a_per_tile = te // N_KEYS
    r = gf.shape[1]
    packed = lambda a: a.reshape(PEER_HEADS, N_KEYS // BF16_ROWS, BF16_ROWS, n)
    key_rows = lambda: pl.BlockSpec((PEER_HEADS, a_per_tile, tn), lambda i, e: (0, e, i))
    key_full = lambda: pl.BlockSpec((PEER_HEADS, N_KEYS // BF16_ROWS, BF16_ROWS, tn), lambda i, e: (0, 0, 0, i))
    return pl.pallas_call(
        functools.partial(_peer_body, a_per_tile=a_per_tile),
        grid=(n // tn, n_e // te),
        in_specs=[pl.BlockSpec((tn, d), lambda i, e: (i, 0)),
                  pl.BlockSpec((te, d), lambda i, e: (e, 0)),
                  pl.BlockSpec((d, te), lambda i, e: (0, e)),
                  key_rows(), key_rows(), key_full(), key_full(),
                  pl.BlockSpec((tn, d), lambda i, e: (i, 0)),
                  pl.BlockSpec((1, r, d), lambda i, e: (i // tiles_per_mod, 0, 0))],
        out_specs=pl.BlockSpec((tn, d), lambda i, e: (i, 0)),
        out_shape=jax.ShapeDtypeStruct((n, d), F32),
        scratch_shapes=[pltpu.VMEM((d, tn), F32), pltpu.VMEM((te, tn), F32), pltpu.VMEM((te, tn), BF16)],
        compiler_params=_cparams(2),
        name="peer_dense",
    )(h2, u_bf, vt_bf, e1, c1, packed(e2), packed(r2), x1, gf)


PEER_EXPERT_TILE = 1024
PEER_TOKEN_TILE = 512


def _layer_group(x, mods, per_token_mods, p, attend, conv_buf, s0, tm, tn_peer):
    b, t, d = x.shape
    n = b * t
    assert n % tm == 0 and n % tn_peer == 0 and (per_token_mods or (t % tm == 0 and t % tn_peer == 0))
    sh_a, sc_a, g_a, sh_f, sc_f, g_f = mods
    tiles_per_seq = t // tm if not per_token_mods else 1
    tiles_per_mod = n // tm if per_token_mods else tiles_per_seq
    xf = x.reshape(n, d)
    q, k, v, kt, vt, lin, z, ba, km = _inproj(xf, sc_a, sh_a, p["norm_attn"], p["w_in"], p["qg"], p["kg"],
                                              p["bd"], tm, tiles_per_mod, tiles_per_seq)
    o_attn = attend(q.reshape(b, t, ATTN_W), k.reshape(b, t, KV_W), v.reshape(b, t, KV_W), vt, km)
    tp = _round_up(t, GDN_CHUNK)
    pad3 = lambda a: jnp.pad(a.reshape(b, t, -1), ((0, 0), (0, tp - t), (0, 0)))
    cbuf8 = jnp.pad(conv_buf, ((0, 0), (SUBLANES - (CONV_W - 1), 0), (0, 0)))
    o_lin, s_new, conv_new = _gdn(pad3(lin), pad3(z), pad3(ba), p["conv_w"], cbuf8, s0,
                                  p["alog"], p["dtb"], p["og"], p["bd"], t)
    o_lin = o_lin[:, :t].reshape(n, LIN_W)
    x1, h2, qt = _outproj(o_attn.reshape(n, ATTN_W), o_lin, xf, g_a, sc_f, sh_f, p["norm_ffn"],
                          p["w_out"], p["wqt"], tm, tiles_per_mod)
    e1, c1, e2, r2 = _peer_route(qt, p["peer_keys"], tm)
    x2 = _peer_dense(h2, p["u_bf"], p["vt_bf"], e1, c1, e2, r2, x1, g_f, tn_peer, PEER_EXPERT_TILE,
                     n // tn_peer if per_token_mods else t // tn_peer)
    to_rows = lambda a: a.reshape(-1, N_KV_HEADS, HEAD_DIM, a.shape[-1]).transpose(0, 3, 1, 2)
    return (x2.reshape(b, t, d), to_rows(kt).reshape(b, t, N_KV_HEADS, HEAD_DIM),
            to_rows(vt).reshape(b, t, N_KV_HEADS, HEAD_DIM), s_new, conv_new)


def kernel(x_prompt, x_sample, cache_k, cache_v, state_ssm, state_conv, page_table, c_prompt, c_sample,
           rel_bias, w_ada, b_ada, norm_attn, norm_ffn, w_in, w_out, q_norm, k_norm, conv_w, a_log,
           dt_bias, o_norm, peer_wq, peer_keys, peer_u, peer_v):
    depth = w_ada.shape[0]
    b, t, d = x_prompt.shape
    b2, s, _ = x_sample.shape
    n_pool = cache_k.shape[1]
    n_pages = page_table.shape[1]
    past = n_pages * PAGE_SIZE
    n_blocks = past // MOBA_BLOCK
    s_pad = 16
    assert t % MOBA_BLOCK == 0 and past % MOBA_BLOCK == 0 and n_pages % PAGES_PER_STEP == 0
    assert MOBA_BLOCK + 1 >= MAX_DISTANCE and s <= s_pad and n_blocks <= LANES
    assert w_in.shape[2] + LANES - 2 * H_LIN == D_IN_PAD

    mod = _ada(jnp.concatenate([c_prompt, c_sample], axis=0), w_ada, b_ada)
    btab = _relbias(rel_bias)
    rb_flat = rel_bias.reshape(-1)

    pool_view = lambda c: c.transpose(0, 1, 3, 4, 2).reshape(depth * n_pool, KV_W, PAGE_SIZE)
    ckt, cvt = pool_view(cache_k), pool_view(cache_v)
    bd = (jnp.arange(ATTN_W)[:, None] // HEAD_DIM == jnp.arange(ATTN_W)[None, :] // HEAD_DIM).astype(BF16)

    rows = N_HEADS * s
    bias_last = btab[1, :, :, :s].transpose(0, 2, 1).reshape(rows, MOBA_BLOCK)
    own = btab[0, :, :s_pad, :s].transpose(0, 2, 1)
    causal = jnp.arange(s_pad)[None, :] <= jnp.arange(s)[:, None]
    bias_own = jnp.where(causal[None], own, NEG_INF).reshape(rows, s_pad)
    far = jnp.repeat(rel_bias[N_BUCKETS - 1], s).reshape(rows, 1)
    kv_of_row = jnp.arange(rows) // (s * GQA)
    lane_kv = jnp.arange(KV_W) // HEAD_DIM
    row_mask = (kv_of_row[:, None] == lane_kv[None, :]).astype(F32)
    nbp = _round_up(t // MOBA_BLOCK, SUBLANES)

    xp, xs = x_prompt, x_sample
    outs = [[] for _ in range(8)]
    for l in range(depth):
        lane_pad = lambda a: jnp.pad(a[l][None, :], ((0, 0), (H_LIN, LANES - 2 * H_LIN)))
        p = {
            "norm_attn": norm_attn[l][None, :], "norm_ffn": norm_ffn[l][None, :],
            "w_in": jnp.pad(w_in[l], ((0, 0), (0, D_IN_PAD - w_in.shape[2]))).astype(BF16),
            "w_out": w_out[l].astype(BF16),
            "qg": jnp.tile(q_norm[l], N_HEADS)[None, :], "kg": jnp.tile(k_norm[l], N_KV_HEADS)[None, :],
            "bd": bd, "conv_w": conv_w[l], "alog": lane_pad(a_log), "dtb": lane_pad(dt_bias),
            "og": o_norm[l][None, :], "wqt": peer_wq[l].T.astype(BF16), "peer_keys": peer_keys[l],
            "u_bf": peer_u[l].astype(BF16), "vt_bf": peer_v[l].T.astype(BF16),
        }
        m6 = jnp.split(mod[l], 6, axis=-1)
        mods_p = [m[:b][:, None, :] for m in m6]
        mods_s = [jnp.repeat(m[b:], s, axis=0)[None] for m in m6]

        def attend_p(q, k, v, vt, km):
            kmean = jnp.pad(km.reshape(b, t // MOBA_BLOCK, KV_W), ((0, 0), (0, nbp - t // MOBA_BLOCK), (0, 0)))
            return _moba_prompt(rb_flat, q, k, vt, kmean, btab)

        pt_flat = (page_table + l * n_pool).reshape(-1).astype(I32)

        def attend_s(q, k, v, vt, km):
            qr = q.reshape(b2, s, N_HEADS, HEAD_DIM).transpose(0, 2, 1, 3).reshape(b2, rows, HEAD_DIM)
            qw = jnp.tile(qr, (1, 1, N_KV_HEADS)) * row_mask[None]
            padn = lambda a: jnp.pad(a, ((0, 0), (0, s_pad - s), (0, 0)))
            o = _moba_sample(pt_flat, ckt, cvt, qw, padn(k), padn(v), bias_last, bias_own, far, n_pages)
            return o.reshape(b2, N_HEADS, s, HEAD_DIM).transpose(0, 2, 1, 3).reshape(b2, s, ATTN_W)

        conv0 = jnp.zeros((b, CONV_W - 1, CONV_CH), F32)
        ssm0 = jnp.zeros((b, H_LIN, HEAD_DIM, HEAD_DIM), F32)
        xp, kp, vp, sp, cp = _layer_group(xp, mods_p, False, p, attend_p, conv0, ssm0, MOBA_BLOCK, PEER_TOKEN_TILE)
        xs, ks, vs, ss, cs = _layer_group(xs, mods_s, True, p, attend_s, state_conv[l], state_ssm[l], b2 * s, b2 * s)
        for lst, val in zip(outs, (kp, vp, ks, vs, sp, ss, cp, cs)):
            lst.append(val)
    return (xp, xs) + tuple(jnp.stack(o) for o in outs)
```

```python
import functools
import math

import numpy as np
import jax
import jax.numpy as jnp
from jax import lax
from jax.experimental import pallas as pl
from jax.experimental.pallas import tpu as pltpu

F32 = jnp.float32
BF16 = jnp.bfloat16
I32 = jnp.int32

HEAD_DIM = 64
N_HEADS = 8
N_KV_HEADS = 4
GQA = N_HEADS // N_KV_HEADS
ATTN_W = N_HEADS * HEAD_DIM
KV_W = N_KV_HEADS * HEAD_DIM
MOBA_BLOCK = 256
MOBA_TOPK = 3
N_BUCKETS = 32
MAX_DISTANCE = 128
H_LIN = 8
LIN_W = H_LIN * HEAD_DIM
CONV_W = 4
CONV_CH = 3 * LIN_W
GDN_CHUNK = 64
GDN_CHUNKS_PER_STEP = 4
INV_BLOCK = 16
PEER_HEADS = 8
N_KEYS = 128
PEER_TOPK = 16
PAGE_SIZE = 128
EPS = 1e-6

LANES = 128
SUBLANES = 8
VMEM_LIMIT_BYTES = 56 * 1024 * 1024

D_IN_PAD = ATTN_W + 2 * KV_W + CONV_CH + LIN_W + LANES
NEG_INF = float("-inf")

_NT = (((1,), (1,)), ((), ()))
_TN = (((0,), (0,)), ((), ()))


def _cparams(n_axes):
    return pltpu.CompilerParams(dimension_semantics=("arbitrary",) * n_axes,
                                vmem_limit_bytes=VMEM_LIMIT_BYTES)


def _round_up(x, m):
    return -(-x // m) * m


def _dot(a, b):
    return jnp.dot(a.astype(BF16), b.astype(BF16), preferred_element_type=F32)


def _dot_nt(a, b):
    return lax.dot_general(a.astype(BF16), b.astype(BF16), _NT, preferred_element_type=F32)


def _dot_tn(a, b):
    return lax.dot_general(a.astype(BF16), b.astype(BF16), _TN, preferred_element_type=F32)


def _bmm(a, b):
    return jnp.einsum("bij,bjk->bik", a.astype(BF16), b.astype(BF16), preferred_element_type=F32)


def _bmm_nt(a, b):
    return jnp.einsum("bik,bjk->bij", a.astype(BF16), b.astype(BF16), preferred_element_type=F32)


def _split2(a):
    hi = a.astype(BF16)
    lo = (a - hi.astype(F32)).astype(BF16)
    return hi, lo


def _split3(a):
    hi = a.astype(BF16)
    r = a - hi.astype(F32)
    mid = r.astype(BF16)
    lo = (r - mid.astype(F32)).astype(BF16)
    return hi, mid, lo


def _dot3(a, b):
    ah, al = _split2(a)
    bh, bl = _split2(b)
    d = functools.partial(jnp.dot, preferred_element_type=F32)
    return d(ah, bh) + (d(al, bh) + d(ah, bl))


def _dot3_nt(a, b):
    ah, al = _split2(a)
    bh, bl = _split2(b)
    d = functools.partial(lax.dot_general, dimension_numbers=_NT, preferred_element_type=F32)
    return d(ah, bh) + (d(al, bh) + d(ah, bl))


def _dot_mask_rhs(a, mask_bf16):
    hi, mid, lo = _split3(a)
    d = functools.partial(jnp.dot, preferred_element_type=F32)
    return d(hi, mask_bf16) + (d(mid, mask_bf16) + d(lo, mask_bf16))


def _dot_mask_lhs(mask_bf16, b):
    hi, mid, lo = _split3(b)
    d = functools.partial(jnp.dot, preferred_element_type=F32)
    return d(mask_bf16, hi) + (d(mask_bf16, mid) + d(mask_bf16, lo))


def _sigmoid(x):
    return 1.0 / (1.0 + jnp.exp(-x))


def _softplus(x):
    return jnp.maximum(x, 0.0) + jnp.log(1.0 + jnp.exp(-jnp.abs(x)))


def _head_sumsq(a, bd):
    return _dot_mask_rhs(a * a, bd)


def _rank_desc(g, idx, n, axis):
    rank = jnp.zeros(g.shape, F32)
    for m in range(n):
        gm = lax.slice_in_dim(g, m, m + 1, axis=axis)
        beats = (gm > g) | ((gm == g) & (idx > m))
        rank = rank + jnp.where(beats, 1.0, 0.0)
    return rank


def _ada_body(c_ref, w_ref, b_ref, o_ref):
    c = c_ref[...]
    s = c * _sigmoid(c)
    o_ref[0] = _dot3(s, w_ref[0]) + b_ref[0]


def _ada(c, w_ada, b_ada):
    n_l, d, d6 = w_ada.shape
    r = c.shape[0]
    tn = 512
    return pl.pallas_call(
        _ada_body,
        grid=(n_l, d6 // tn),
        in_specs=[pl.BlockSpec((r, d), lambda l, j: (0, 0)),
                  pl.BlockSpec((1, d, tn), lambda l, j: (l, 0, j)),
                  pl.BlockSpec((1, 1, tn), lambda l, j: (l, 0, j))],
        out_specs=pl.BlockSpec((1, r, tn), lambda l, j: (l, 0, j)),
        out_shape=jax.ShapeDtypeStruct((n_l, r, d6), F32),
        compiler_params=_cparams(2),
        name="ada",
    )(c, w_ada, b_ada.reshape(n_l, 1, d6))


def _inproj_body(x_ref, sc_ref, sh_ref, gn_ref, w_ref, qg_ref, kg_ref, bd_ref,
                 q_ref, k_ref, v_ref, kt_ref, vt_ref, lin_ref, z_ref, ba_ref, km_ref):
    x = x_ref[...]
    ms = jnp.mean(x * x, axis=-1, keepdims=True)
    h = x * lax.rsqrt(ms + EPS) * gn_ref[...]
    h = h * (1.0 + sc_ref[0]) + sh_ref[0]
    proj = jnp.dot(h.astype(BF16), w_ref[...], preferred_element_type=F32)
    bd = bd_ref[...]
    aq = proj[:, :ATTN_W]
    ak = proj[:, ATTN_W:ATTN_W + KV_W]
    q = aq * lax.rsqrt(_head_sumsq(aq, bd) * (1.0 / HEAD_DIM) + EPS) * qg_ref[...]
    k = ak * lax.rsqrt(_head_sumsq(ak, bd[:KV_W, :KV_W]) * (1.0 / HEAD_DIM) + EPS) * kg_ref[...]
    q_ref[...] = q * (HEAD_DIM ** -0.5)
    k_ref[...] = k
    kt_ref[0] = k.T
    o = ATTN_W + KV_W
    v = proj[:, o:o + KV_W]
    v_ref[...] = v
    vt_ref[0] = v.T
    o += KV_W
    lin_ref[...] = proj[:, o:o + CONV_CH]
    o += CONV_CH
    z_ref[...] = proj[:, o:o + LIN_W]
    o += LIN_W
    ba_ref[...] = proj[:, o:o + LANES]
    n_km = km_ref.shape[1]
    rows_per = k.shape[0] // n_km
    for j in range(n_km):
        km_ref[0, j:j + 1, :] = jnp.mean(k[j * rows_per:(j + 1) * rows_per], axis=0, keepdims=True)


def _inproj(x, sc, sh, gn, w_pad, qg, kg, bd, tm, tiles_per_mod, tiles_per_seq):
    n, d = x.shape
    r = sc.shape[1]
    nt = n // tm
    n_seq = nt // tiles_per_seq
    mod_spec = pl.BlockSpec((1, r, d), lambda i: (i // tiles_per_mod, 0, 0))
    const = lambda shape: pl.BlockSpec(shape, lambda i: (0,) * len(shape))
    row = lambda w: pl.BlockSpec((tm, w), lambda i: (i, 0))
    tspec = pl.BlockSpec((1, KV_W, tm), lambda i: (i // tiles_per_seq, 0, i % tiles_per_seq))
    tshape = jax.ShapeDtypeStruct((n_seq, KV_W, tiles_per_seq * tm), F32)
    n_km = max(tm // MOBA_BLOCK, 1)
    return pl.pallas_call(
        _inproj_body,
        grid=(nt,),
        in_specs=[row(d), mod_spec, mod_spec, const((1, d)), const(w_pad.shape),
                  const((1, ATTN_W)), const((1, KV_W)), const((ATTN_W, ATTN_W))],
        out_specs=[row(ATTN_W), row(KV_W), row(KV_W), tspec, tspec, row(CONV_CH), row(LIN_W), row(LANES),
                   pl.BlockSpec((1, n_km, KV_W), lambda i: (i, 0, 0))],
        out_shape=[jax.ShapeDtypeStruct((n, ATTN_W), F32), jax.ShapeDtypeStruct((n, KV_W), F32),
                   jax.ShapeDtypeStruct((n, KV_W), F32), tshape, tshape,
                   jax.ShapeDtypeStruct((n, CONV_CH), F32),
                   jax.ShapeDtypeStruct((n, LIN_W), F32), jax.ShapeDtypeStruct((n, LANES), F32),
                   jax.ShapeDtypeStruct((nt, n_km, KV_W), F32)],
        compiler_params=_cparams(1),
        name="inproj",
    )(x, sc, sh, gn, w_pad, qg, kg, bd)


def _relbias_body(rb_ref, o_ref):
    n_tab, n_h, r, c = o_ref.shape
    key = lax.broadcasted_iota(I32, (r, c), 0)
    qry = lax.broadcasted_iota(I32, (r, c), 1)
    max_exact = N_BUCKETS // 2
    for t in range(n_tab):
        dist = jnp.maximum(qry - key + t * MOBA_BLOCK, 0)
        nf = jnp.maximum(dist, 1).astype(F32)
        large = max_exact + (jnp.log(nf / max_exact) / math.log(MAX_DISTANCE / max_exact)
                             * (N_BUCKETS - max_exact)).astype(I32)
        bucket = jnp.where(dist < max_exact, dist, jnp.minimum(large, N_BUCKETS - 1))

        def per_head(h, _, bucket=bucket, t=t):
            acc = jnp.zeros((r, c), F32)
            for j in range(N_BUCKETS):
                acc = jnp.where(bucket == j, rb_ref[j * n_h + h], acc)
            o_ref[t, h] = acc
            return 0

        lax.fori_loop(0, n_h, per_head, 0)


def _relbias(rel_bias):
    n_h = rel_bias.shape[1]
    return pl.pallas_call(
        _relbias_body,
        in_specs=[pl.BlockSpec(memory_space=pltpu.SMEM)],
        out_specs=pl.BlockSpec(memory_space=pltpu.VMEM),
        out_shape=jax.ShapeDtypeStruct((2, n_h, MOBA_BLOCK, MOBA_BLOCK), F32),
        compiler_params=pltpu.CompilerParams(vmem_limit_bytes=VMEM_LIMIT_BYTES),
        name="relbias",
    )(rel_bias.reshape(-1))


def _moba_prompt_body(sch_ref, rb_ref, q_ref, km_ref, k_ref, vt_ref, bown_ref, bprev_ref, o_ref,
                      selt, m_s, l_s, acct):
    pair = pl.program_id(1)
    i = sch_ref[pair]
    n = sch_ref[pl.num_programs(1) + pair]
    nbp = km_ref.shape[1]
    blk = q_ref.shape[1]
    far_bucket = N_BUCKETS - 1

    def hs(h):
        return slice(h * HEAD_DIM, (h + 1) * HEAD_DIM)

    def group_q(kv):
        return jnp.concatenate([q_ref[0, :, hs(kv * GQA + g)] for g in range(GQA)], axis=0)

    def group_lanes(fn, kv):
        return jnp.concatenate([fn(kv * GQA + g) for g in range(GQA)], axis=1)

    def scores(kv):
        return _dot_nt(k_ref[0, :, hs(kv)], group_q(kv))

    def vtk(kv):
        return vt_ref[0, kv * HEAD_DIM:(kv + 1) * HEAD_DIM, :]

    @pl.when(n == 0)
    def _first():
        row = lax.broadcasted_iota(I32, (nbp, GQA * blk), 0)
        kr = lax.broadcasted_iota(I32, (blk, blk), 0)
        qc = lax.broadcasted_iota(I32, (blk, blk), 1)
        causal = jnp.where(kr <= qc, 0.0, NEG_INF)
        for kv in range(N_KV_HEADS):
            gate = _dot3_nt(km_ref[0, :, hs(kv)], group_q(kv))
            g = jnp.where(row < i, gate, NEG_INF)
            rank = _rank_desc(g, row, nbp, 0)
            selt[kv] = jnp.where((rank < MOBA_TOPK) & (row < i), 1.0, 0.0)
            s = scores(kv) + group_lanes(lambda h: bown_ref[h] + causal, kv)
            m = jnp.max(s, axis=0, keepdims=True)
            p = jnp.exp(s - m)
            m_s[kv] = m
            l_s[kv] = jnp.sum(p, axis=0, keepdims=True)
            acct[kv] = _dot(vtk(kv), p)

    @pl.when(n >= 1)
    def _past():
        kb = n - 1
        is_prev = kb == i - 1
        for kv in range(N_KV_HEADS):
            far = group_lanes(lambda h: jnp.full((1, blk), rb_ref[far_bucket * N_HEADS + h], F32), kv)
            bias = jnp.where(is_prev, group_lanes(lambda h: bprev_ref[h], kv), far)
            sel = selt[kv, pl.ds(kb, 1), :]
            s = jnp.where(sel > 0.5, scores(kv) + bias, NEG_INF)
            m_old = m_s[kv]
            m_new = jnp.maximum(m_old, jnp.max(s, axis=0, keepdims=True))
            alpha = jnp.exp(m_old - m_new)
            p = jnp.exp(s - m_new)
            m_s[kv] = m_new
            l_s[kv] = alpha * l_s[kv] + jnp.sum(p, axis=0, keepdims=True)
            acct[kv] = alpha * acct[kv] + _dot(vtk(kv), p)

    @pl.when(n == i)
    def _done():
        outs = []
        for kv in range(N_KV_HEADS):
            a = acct[kv] / l_s[kv]
            outs += [a[:, g * blk:(g + 1) * blk] for g in range(GQA)]
        o_ref[0] = jnp.concatenate(outs, axis=0).T


def _moba_prompt(rb_flat, q, k, vt, kmean, btab):
    b, t, _ = q.shape
    nb = t // MOBA_BLOCK
    nbp = kmean.shape[1]
    blk = MOBA_BLOCK

    pairs = [(i, n) for i in range(nb) for n in range(i + 1)]
    n_pairs = len(pairs)
    sched = jnp.asarray([p[0] for p in pairs] + [p[1] for p in pairs], I32)

    def q_blk(pr, sch):
        return sch[pr]

    def kv_blk(pr, sch):
        i, n = sch[pr], sch[n_pairs + pr]
        return jnp.where(n == 0, i, n - 1)

    return pl.pallas_call(
        _moba_prompt_body,
        grid_spec=pltpu.PrefetchScalarGridSpec(
            num_scalar_prefetch=1,
            grid=(b, n_pairs),
            in_specs=[pl.BlockSpec(memory_space=pltpu.SMEM),
                      pl.BlockSpec((1, blk, ATTN_W), lambda bi, pr, sch: (bi, q_blk(pr, sch), 0)),
                      pl.BlockSpec((1, nbp, KV_W), lambda bi, pr, sch: (bi, 0, 0)),
                      pl.BlockSpec((1, blk, KV_W), lambda bi, pr, sch: (bi, kv_blk(pr, sch), 0)),
                      pl.BlockSpec((1, KV_W, blk), lambda bi, pr, sch: (bi, 0, kv_blk(pr, sch))),
                      pl.BlockSpec((None, N_HEADS, blk, blk), lambda bi, pr, sch: (0, 0, 0, 0)),
                      pl.BlockSpec((None, N_HEADS, blk, blk), lambda bi, pr, sch: (1, 0, 0, 0))],
            out_specs=pl.BlockSpec((1, blk, ATTN_W), lambda bi, pr, sch: (bi, q_blk(pr, sch), 0)),
            scratch_shapes=[pltpu.VMEM((N_KV_HEADS, nbp, GQA * blk), F32),
                            pltpu.VMEM((N_KV_HEADS, 1, GQA * blk), F32),
                            pltpu.VMEM((N_KV_HEADS, 1, GQA * blk), F32),
                            pltpu.VMEM((N_KV_HEADS, HEAD_DIM, GQA * blk), F32)]),
        out_shape=jax.ShapeDtypeStruct((b, t, ATTN_W), F32),
        compiler_params=_cparams(2),
        name="moba_prompt",
    )(sched, rb_flat, q, kmean, k, vt, btab, btab)


PAGES_PER_STEP = 32
PAGES_PER_BLOCK = MOBA_BLOCK // PAGE_SIZE
BLOCKS_PER_STEP = PAGES_PER_STEP // PAGES_PER_BLOCK


def _moba_sample_body(pt_ref, qw_ref, kn_ref, vn_ref, bl_ref, bo_ref, far_ref, *refs, n_blocks):
    kp = refs[:PAGES_PER_STEP]
    vp = refs[PAGES_PER_STEP:2 * PAGES_PER_STEP]
    o_ref, kres, kmt, selc, m_s, l_s, acc = refs[2 * PAGES_PER_STEP:]
    ph = pl.program_id(1)
    j = pl.program_id(2)
    n_steps = pl.num_programs(2)
    rows = qw_ref.shape[1]
    qw = qw_ref[0]
    qw_b = qw.astype(BF16)

    @pl.when(ph == 0)
    def _scan_keys():
        lane = lax.broadcasted_iota(I32, (KV_W, LANES), 1)

        @pl.when(j == 0)
        def _zero():
            kmt[...] = jnp.zeros(kmt.shape, F32)

        cur = kmt[...]
        for nn in range(BLOCKS_PER_STEP):
            tot = kp[nn * PAGES_PER_BLOCK][0]
            for p in range(1, PAGES_PER_BLOCK):
                tot = tot + kp[nn * PAGES_PER_BLOCK + p][0]
            col = jnp.sum(tot, axis=1, keepdims=True) * (1.0 / MOBA_BLOCK)
            cur = jnp.where(lane == j * BLOCKS_PER_STEP + nn, col, cur)
            for p in range(PAGES_PER_BLOCK):
                r = nn * PAGES_PER_BLOCK + p
                kres[j * PAGES_PER_STEP + r] = kp[r][0].astype(BF16)
        kmt[...] = cur

    @pl.when((ph == 1) & (j == 0))
    def _select():
        gate = _dot3(qw, kmt[...])
        col = lax.broadcasted_iota(I32, (rows, LANES), 1)
        g = jnp.where(col < n_blocks, gate, NEG_INF)
        rank = _rank_desc(g, col, n_blocks, 1)
        sel = jnp.where((rank < MOBA_TOPK) & (col < n_blocks), 1.0, 0.0)
        for jj in range(n_blocks // BLOCKS_PER_STEP):
            selc[jj] = sel[:, jj * BLOCKS_PER_STEP:(jj + 1) * BLOCKS_PER_STEP]
        s = _dot_nt(qw_b, kn_ref[0]) + bo_ref[...]
        m = jnp.max(s, axis=-1, keepdims=True)
        p = jnp.exp(s - m)
        m_s[...] = m
        l_s[...] = jnp.sum(p, axis=-1, keepdims=True)
        acc[...] = _dot(p, vn_ref[0])

    @pl.when(ph == 1)
    def _attend():
        sel_j = selc[j]
        pieces = []
        for nn in range(BLOCKS_PER_STEP):
            s = jnp.concatenate(
                [jnp.dot(qw_b, kres[j * PAGES_PER_STEP + nn * PAGES_PER_BLOCK + p],
                         preferred_element_type=F32) for p in range(PAGES_PER_BLOCK)], axis=1)
            if nn == BLOCKS_PER_STEP - 1:
                s = s + jnp.where(j == n_steps - 1, bl_ref[...], far_ref[...])
            else:
                s = s + far_ref[...]
            pieces.append(jnp.where(sel_j[:, nn:nn + 1] > 0.5, s, NEG_INF))
        s_all = jnp.concatenate(pieces, axis=1)
        m_old = m_s[...]
        m_new = jnp.maximum(m_old, jnp.max(s_all, axis=-1, keepdims=True))
        alpha = jnp.exp(m_old - m_new)
        p_all = jnp.exp(s_all - m_new)
        m_s[...] = m_new
        l_s[...] = alpha * l_s[...] + jnp.sum(p_all, axis=-1, keepdims=True)
        p_b = p_all.astype(BF16)
        pv = jnp.zeros(acc.shape, F32)
        for r in range(PAGES_PER_STEP):
            pv = pv + lax.dot_general(p_b[:, r * PAGE_SIZE:(r + 1) * PAGE_SIZE], vp[r][0].astype(BF16),
                                      _NT, preferred_element_type=F32)
        acc[...] = alpha * acc[...] + pv

    @pl.when((ph == 1) & (j == n_steps - 1))
    def _done():
        out = acc[...] / l_s[...]
        rpk = rows // N_KV_HEADS
        for kv in range(N_KV_HEADS):
            o_ref[0, kv * rpk:(kv + 1) * rpk, :] = out[kv * rpk:(kv + 1) * rpk,
                                                      kv * HEAD_DIM:(kv + 1) * HEAD_DIM]


def _moba_sample(pt_flat, ckt, cvt, qw, kn, vn, bias_last, bias_own, far, n_pages):
    b2, rows, _ = qw.shape
    n_blocks = n_pages // PAGES_PER_BLOCK
    n_steps = n_pages // PAGES_PER_STEP
    s_pad = kn.shape[1]
    last = n_steps - 1

    def k_spec(r):
        return pl.BlockSpec((1, KV_W, PAGE_SIZE), lambda b, ph, j, pt: (
            pt[b * n_pages + jnp.where(ph == 0, j, last) * PAGES_PER_STEP + r], 0, 0))

    def v_spec(r):
        return pl.BlockSpec((1, KV_W, PAGE_SIZE), lambda b, ph, j, pt: (
            pt[b * n_pages + jnp.where(ph == 0, 0, j) * PAGES_PER_STEP + r], 0, 0))

    per_b = lambda shape: pl.BlockSpec((1,) + shape, lambda b, ph, j, pt: (b, 0, 0))
    const = lambda shape: pl.BlockSpec(shape, lambda b, ph, j, pt: (0, 0))
    return pl.pallas_call(
        functools.partial(_moba_sample_body, n_blocks=n_blocks),
        grid_spec=pltpu.PrefetchScalarGridSpec(
            num_scalar_prefetch=1,
            grid=(b2, 2, n_steps),
            in_specs=[per_b((rows, KV_W)), per_b((s_pad, KV_W)), per_b((s_pad, KV_W)),
                      const((rows, MOBA_BLOCK)), const((rows, s_pad)), const((rows, 1))]
            + [k_spec(r) for r in range(PAGES_PER_STEP)] + [v_spec(r) for r in range(PAGES_PER_STEP)],
            out_specs=pl.BlockSpec((1, rows, HEAD_DIM), lambda b, ph, j, pt: (b, 0, 0)),
            scratch_shapes=[pltpu.VMEM((n_pages, KV_W, PAGE_SIZE), BF16),
                            pltpu.VMEM((KV_W, LANES), F32),
                            pltpu.VMEM((n_steps, rows, BLOCKS_PER_STEP), F32),
                            pltpu.VMEM((rows, 1), F32),
                            pltpu.VMEM((rows, 1), F32),
                            pltpu.VMEM((rows, KV_W), F32)]),
        out_shape=jax.ShapeDtypeStruct((b2, rows, HEAD_DIM), F32),
        compiler_params=_cparams(3),
        name="moba_sample",
    )(pt_flat, qw, kn, vn, bias_last, bias_own, far,
      *([ckt] * PAGES_PER_STEP), *([cvt] * PAGES_PER_STEP))


def _gdn_body(lin_ref, z_ref, ba_ref, cw_ref, cb_ref, s0_ref, al_ref, dtb_ref, og_ref, bd_ref,
              o_ref, sout_ref, cout_ref, xbuf, state, *, t_valid, nc):
    t = pl.program_id(1)
    n_t = pl.num_programs(1)
    c = GDN_CHUNK
    tt = nc * c
    halo = SUBLANES

    @pl.when(t == 0)
    def _init():
        xbuf[0:halo, :] = cb_ref[0]
        state[...] = s0_ref[0]

    @pl.when(t > 0)
    def _carry():
        xbuf[0:halo, :] = xbuf[tt:tt + halo, :]

    xbuf[halo:halo + tt, :] = lin_ref[0]
    first = halo - (CONV_W - 1)
    conv = cw_ref[0:1, :] * xbuf[pl.ds(first, tt), :]
    for i in range(1, CONV_W):
        conv = conv + cw_ref[i:i + 1, :] * xbuf[pl.ds(first + i, tt), :]
    a = conv * _sigmoid(conv)
    lq = a[:, :LIN_W]
    lk = a[:, LIN_W:2 * LIN_W]
    lv = a[:, 2 * LIN_W:]
    bd = bd_ref[...]
    lq = lq * lax.rsqrt(_head_sumsq(lq, bd) + EPS) * (HEAD_DIM ** -0.5)
    lk = lk * lax.rsqrt(_head_sumsq(lk, bd) + EPS)
    ba = ba_ref[0]
    beta_all = _sigmoid(ba)
    g_all = -jnp.exp(al_ref[...]) * _softplus(ba + dtb_ref[...])
    if t_valid % tt != 0:
        rid = t * tt + lax.broadcasted_iota(I32, (tt, 1), 0)
        ok = rid < t_valid
        lq = jnp.where(ok, lq, 0.0)
        lk = jnp.where(ok, lk, 0.0)
        lv = jnp.where(ok, lv, 0.0)
        beta_all = jnp.where(ok, beta_all, 0.0)
        g_all = jnp.where(ok, g_all, 0.0)

    it = lax.broadcasted_iota(I32, (tt, tt), 0)
    jt = lax.broadcasted_iota(I32, (tt, tt), 1)
    tril = jnp.where((it >= jt) & (it // c == jt // c), 1.0, 0.0).astype(BF16)
    gc_all = _dot_mask_lhs(tril, g_all)

    def heads(x):
        return jnp.stack([x[ci * c:(ci + 1) * c, h * HEAD_DIM:(h + 1) * HEAD_DIM]
                          for ci in range(nc) for h in range(H_LIN)], axis=0)

    def cols(x, off):
        return jnp.stack([x[ci * c:(ci + 1) * c, off + h:off + h + 1]
                          for ci in range(nc) for h in range(H_LIN)], axis=0)

    q, k, v = heads(lq), heads(lk), heads(lv)
    beta = cols(beta_all, 0)
    gc = cols(gc_all, H_LIN)
    ii = lax.broadcasted_iota(I32, (1, c, c), 1)
    jj = lax.broadcasted_iota(I32, (1, c, c), 2)
    lower = ii >= jj
    strict = ii > jj
    diag_blk = (ii // INV_BLOCK) == (jj // INV_BLOCK)
    gc_row = jnp.sum(jnp.where(ii == jj, gc, 0.0), axis=1, keepdims=True)
    eg = jnp.exp(gc)
    dmat = jnp.where(lower, jnp.exp(jnp.minimum(gc - gc_row, 0.0)), 0.0)
    kb = k * beta
    vb = v * beta
    lm = _bmm_nt(kb, k) * jnp.where(strict, dmat, 0.0)
    dm = jnp.where(diag_blk, lm, 0.0)
    nm = lm - dm
    xp = -dm
    p = _bmm(dm, dm)
    steps = int(math.log2(INV_BLOCK)) - 1
    for s in range(steps):
        xp = xp + p + _bmm(xp, p)
        if s + 1 < steps:
            p = _bmm(p, p)
    mm = nm + _bmm(xp, nm)
    yp = -mm
    pm = _bmm(mm, mm)
    msteps = int(math.log2(c // INV_BLOCK)) - 1
    for s in range(msteps):
        yp = yp + pm + _bmm(yp, pm)
        if s + 1 < msteps:
            pm = _bmm(pm, pm)
    tp = yp + xp + _bmm(yp, xp)
    kbg = kb * eg
    u = vb + _bmm(tp, vb)
    w = kbg + _bmm(tp, kbg)
    qk = _bmm_nt(q, k) * dmat
    qd = q * eg
    gl = gc[:, c - 1:c, :]
    kt = k * jnp.exp(gl - gc)
    egl = jnp.exp(gl)

    s_cur = state[...]
    for ci in range(nc):
        sl = slice(ci * H_LIN, (ci + 1) * H_LIN)
        v_new = u[sl] - _bmm(w[sl], s_cur)
        o = _bmm(qd[sl], s_cur) + _bmm(qk[sl], v_new)
        upd = jnp.stack([_dot_tn(kt[ci * H_LIN + h], v_new[h]) for h in range(H_LIN)], axis=0)
        s_cur = s_cur * egl[sl] + upd
        on = o * lax.rsqrt(jnp.mean(o * o, axis=-1, keepdims=True) + EPS) * og_ref[...]
        for h in range(H_LIN):
            zh = z_ref[0, ci * c:(ci + 1) * c, h * HEAD_DIM:(h + 1) * HEAD_DIM]
            o_ref[0, ci * c:(ci + 1) * c, h * HEAD_DIM:(h + 1) * HEAD_DIM] = on[h] * (zh * _sigmoid(zh))
    state[...] = s_cur

    @pl.when(t == n_t - 1)
    def _fin():
        sout_ref[0] = state[...]
        tv = t_valid - (t_valid - 1) // tt * tt
        cout_ref[0] = xbuf[pl.ds(halo + tv - (CONV_W - 1), CONV_W - 1), :]


def _gdn(lin, z, ba, conv_w, cbuf8, s0, alog_l, dtb_l, og, bd, t_valid):
    b, tp, _ = lin.shape
    c = GDN_CHUNK
    nc = GDN_CHUNKS_PER_STEP if tp % (GDN_CHUNKS_PER_STEP * c) == 0 else 1
    tt = nc * c
    n_t = tp // tt
    tile = lambda w: pl.BlockSpec((1, tt, w), lambda bi, t: (bi, t, 0))
    const = lambda shape: pl.BlockSpec(shape, lambda bi, t: (0,) * len(shape))
    return pl.pallas_call(
        functools.partial(_gdn_body, t_valid=t_valid, nc=nc),
        grid=(b, n_t),
        in_specs=[tile(CONV_CH), tile(LIN_W), tile(LANES), const((CONV_W, CONV_CH)),
                  pl.BlockSpec((1, SUBLANES, CONV_CH), lambda bi, t: (bi, 0, 0)),
                  pl.BlockSpec((1, H_LIN, HEAD_DIM, HEAD_DIM), lambda bi, t: (bi, 0, 0, 0)),
                  const((1, LANES)), const((1, LANES)), const((1, HEAD_DIM)), const((LIN_W, LIN_W))],
        out_specs=[tile(LIN_W),
                   pl.BlockSpec((1, H_LIN, HEAD_DIM, HEAD_DIM), lambda bi, t: (bi, 0, 0, 0)),
                   pl.BlockSpec((1, CONV_W - 1, CONV_CH), lambda bi, t: (bi, 0, 0))],
        out_shape=[jax.ShapeDtypeStruct((b, tp, LIN_W), F32),
                   jax.ShapeDtypeStruct((b, H_LIN, HEAD_DIM, HEAD_DIM), F32),
                   jax.ShapeDtypeStruct((b, CONV_W - 1, CONV_CH), F32)],
        scratch_shapes=[pltpu.VMEM((tt + 2 * SUBLANES, CONV_CH), F32),
                        pltpu.VMEM((H_LIN, HEAD_DIM, HEAD_DIM), F32)],
        compiler_params=_cparams(2),
        name="gdn",
    )(lin, z, ba, conv_w, cbuf8, s0, alog_l, dtb_l, og, bd)


def _outproj_body(oa_ref, ol_ref, x_ref, ga_ref, sc_ref, sh_ref, gn_ref, wo_ref, wqt_ref,
                  x1_ref, h2_ref, qt_ref):
    y = (jnp.dot(oa_ref[...].astype(BF16), wo_ref[:ATTN_W, :], preferred_element_type=F32)
         + jnp.dot(ol_ref[...].astype(BF16), wo_ref[ATTN_W:, :], preferred_element_type=F32))
    x1 = x_ref[...] + ga_ref[0] * y
    ms = jnp.mean(x1 * x1, axis=-1, keepdims=True)
    h2 = x1 * lax.rsqrt(ms + EPS) * gn_ref[...]
    h2 = (h2 * (1.0 + sc_ref[0]) + sh_ref[0]).astype(BF16)
    x1_ref[...] = x1
    h2_ref[...] = h2
    qt_ref[...] = lax.dot_general(wqt_ref[...], h2, _NT, preferred_element_type=F32)


def _outproj(oa, ol, x, ga, sc, sh, gn, wo, wqt, tm, tiles_per_mod):
    n, d = x.shape
    r = sc.shape[1]
    pq = wqt.shape[0]
    mod_spec = pl.BlockSpec((1, r, d), lambda i: (i // tiles_per_mod, 0, 0))
    const = lambda shape: pl.BlockSpec(shape, lambda i: (0,) * len(shape))
    row = lambda w: pl.BlockSpec((tm, w), lambda i: (i, 0))
    return pl.pallas_call(
        _outproj_body,
        grid=(n // tm,),
        in_specs=[row(ATTN_W), row(LIN_W), row(d), mod_spec, mod_spec, mod_spec, const((1, d)),
                  const(wo.shape), const(wqt.shape)],
        out_specs=[row(d), row(d), pl.BlockSpec((pq, tm), lambda i: (0, i))],
        out_shape=[jax.ShapeDtypeStruct((n, d), F32), jax.ShapeDtypeStruct((n, d), BF16),
                   jax.ShapeDtypeStruct((pq, n), F32)],
        compiler_params=_cparams(1),
        name="outproj",
    )(oa, ol, x, ga, sc, sh, gn, wo, wqt)


_CAND = [(r1, r2) for r1 in range(PEER_TOPK) for r2 in range(PEER_TOPK) if (r1 + 1) * (r2 + 1) <= PEER_TOPK]
_N_CAND_PAD = _round_up(len(_CAND), 16)


def _cand_select_mats():
    m = np.zeros((2, _N_CAND_PAD, PEER_TOPK), np.float32)
    for p, (r1, r2) in enumerate(_CAND):
        m[0, p, r1] = 1.0
        m[1, p, r2] = 1.0
    return jnp.asarray(m, dtype=BF16)


def _topk_rows(s, k, tie_exact):
    rows, t = s.shape
    kio = lax.broadcasted_iota(I32, (k, t), 0)
    cur = s
    rank = jnp.full((rows, t), float(k), F32)
    vals = jnp.zeros((k, t), F32)
    if tie_exact:
        iota = lax.broadcasted_iota(I32, (rows, t), 0).astype(F32)
    for j in range(k):
        m = jnp.max(cur, axis=0, keepdims=True)
        hit = cur == m
        if tie_exact:
            idx = jnp.min(jnp.where(hit, iota, float(rows)), axis=0, keepdims=True)
            hit = iota == idx
        rank = jnp.where(hit, float(j), rank)
        cur = jnp.where(hit, NEG_INF, cur)
        vals = jnp.where(kio == j, m, vals)
    removed = jnp.sum(jnp.where(rank < k, 1.0, 0.0), axis=0, keepdims=True)
    return vals, rank, removed


def _route_head(s1, s2, csel_ref, tie_exact):
    kk = PEER_TOPK
    tn = s1.shape[1]
    crow = lax.broadcasted_iota(I32, (_N_CAND_PAD, tn), 0)
    v1, rank1, n1 = _topk_rows(s1, kk, tie_exact)
    v2, rank2, n2 = _topk_rows(s2, kk, tie_exact)
    cand = _dot_mask_lhs(csel_ref[0], v1) + _dot_mask_lhs(csel_ref[1], v2)
    cand = jnp.where(crow < len(_CAND), cand, NEG_INF)
    cv, crank, nc = _topk_rows(cand, kk, tie_exact)
    per_rank = _dot_tn(csel_ref[0], jnp.where(crank < kk, 1.0, 0.0))
    cnt1 = jnp.zeros(s1.shape, F32)
    for j in range(kk):
        cnt1 = jnp.where(rank1 == float(j), per_rank[j:j + 1, :], cnt1)
    z = jnp.sum(jnp.exp(cv - cv[0:1, :]), axis=0, keepdims=True)
    w1 = jnp.where(rank1 < kk, jnp.exp(s1 - v1[0:1, :]), 0.0) * (0.5 / z)
    w2 = jnp.where(rank2 < kk, jnp.exp(s2 - v2[0:1, :]), 0.0)
    excess = jnp.max(jnp.abs(n1 - kk) + jnp.abs(n2 - kk) + jnp.abs(nc - kk))
    return w1, cnt1, w2, rank2, excess


def _route_body(qt_ref, keys_ref, csel_ref, e1_ref, c1_ref, e2_ref, r2_ref):
    half = keys_ref.shape[3]

    def per_head(h, _):
        s1 = _dot3(keys_ref[h, 0], qt_ref[pl.ds(pl.multiple_of(h * 2 * half, half), half), :])
        s2 = _dot3(keys_ref[h, 1], qt_ref[pl.ds(pl.multiple_of(h * 2 * half + half, half), half), :])

        def emit(tie_exact):
            w1, cnt1, w2, rank2, excess = _route_head(s1, s2, csel_ref, tie_exact)
            e1_ref[h] = w1
            c1_ref[h] = cnt1
            e2_ref[h] = w2.astype(BF16)
            r2_ref[h] = rank2.astype(BF16)
            return excess

        excess = emit(False)

        @pl.when(excess > 0.0)
        def _redo():
            emit(True)

        return 0

    lax.fori_loop(0, PEER_HEADS, per_head, 0)


def _peer_route(qt, keys, tn):
    pq, n = qt.shape
    csel = _cand_select_mats()
    out = lambda dt: jax.ShapeDtypeStruct((PEER_HEADS, N_KEYS, n), dt)
    ospec = pl.BlockSpec((PEER_HEADS, N_KEYS, tn), lambda i: (0, 0, i))
    return pl.pallas_call(
        _route_body,
        grid=(n // tn,),
        in_specs=[pl.BlockSpec((pq, tn), lambda i: (0, i)),
                  pl.BlockSpec(keys.shape, lambda i: (0, 0, 0, 0)),
                  pl.BlockSpec(csel.shape, lambda i: (0, 0, 0))],
        out_specs=[ospec] * 4,
        out_shape=[out(F32), out(F32), out(BF16), out(BF16)],
        compiler_params=_cparams(1),
        name="peer_route",
    )(qt, keys, csel)


BF16_ROWS = 2 * SUBLANES


def _peer_body(h2_ref, u_ref, vt_ref, e1_ref, c1_ref, e2_ref, r2_ref, x1_ref, gf_ref, o_ref,
               acc, pre_s, ga, *, a_per_tile):
    e = pl.program_id(1)
    tn = h2_ref.shape[0]

    @pl.when(e == 0)
    def _zero():
        acc[...] = jnp.zeros(acc.shape, F32)

    mm_blocks = 4
    mrows = a_per_tile * N_KEYS // mm_blocks
    for al in range(a_per_tile):
        if al % (a_per_tile // mm_blocks) == 0:
            bi = al // (a_per_tile // mm_blocks)
            pre_s[bi * mrows:(bi + 1) * mrows, :] = lax.dot_general(
                u_ref[bi * mrows:(bi + 1) * mrows, :], h2_ref[...], _NT, preferred_element_type=F32)
        g = jnp.zeros((N_KEYS // BF16_ROWS, BF16_ROWS, tn), BF16)
        for h in range(PEER_HEADS):
            cnt = jnp.broadcast_to(c1_ref[h, al:al + 1, :], (BF16_ROWS, tn)).astype(BF16)
            w1 = jnp.broadcast_to(e1_ref[h, al:al + 1, :], (BF16_ROWS, tn)).astype(BF16)
            hit = r2_ref[h] < cnt[None]
            g = g + jnp.where(hit, e2_ref[h], jnp.zeros((), BF16)) * w1[None]
        ga[al * N_KEYS:(al + 1) * N_KEYS, :] = g.reshape(N_KEYS, tn)
    n_part = 2
    part = a_per_tile * N_KEYS // n_part
    for pi in range(n_part):
        prows = slice(pi * part, (pi + 1) * part)
        pre = pre_s[prows, :]
        act = pre * (1.0 + lax.erf(pre * (2.0 ** -0.5)))
        gact = ga[prows, :] * act.astype(BF16)
        acc[...] += jnp.dot(vt_ref[:, prows], gact, preferred_element_type=F32)

    @pl.when(e == pl.num_programs(1) - 1)
    def _done():
        o_ref[...] = x1_ref[...] + gf_ref[0] * acc[...].T


def _peer_dense(h2, u_bf, vt_bf, e1, c1, e2, r2, x1, gf, tn, te, tiles_per_mod):
    n, d = x1.shape
    n_e = u_bf.shape[0]
    a_per_tile = te // N_KEYS
    r = gf.shape[1]
    packed = lambda a: a.reshape(PEER_HEADS, N_KEYS // BF16_ROWS, BF16_ROWS, n)
    key_rows = lambda: pl.BlockSpec((PEER_HEADS, a_per_tile, tn), lambda i, e: (0, e, i))
    key_full = lambda: pl.BlockSpec((PEER_HEADS, N_KEYS // BF16_ROWS, BF16_ROWS, tn), lambda i, e: (0, 0, 0, i))
    return pl.pallas_call(
        functools.partial(_peer_body, a_per_tile=a_per_tile),
        grid=(n // tn, n_e // te),
        in_specs=[pl.BlockSpec((tn, d), lambda i, e: (i, 0)),
                  pl.BlockSpec((te, d), lambda i, e: (e, 0)),
                  pl.BlockSpec((d, te), lambda i, e: (0, e)),
                  key_rows(), key_rows(), key_full(), key_full(),
                  pl.BlockSpec((tn, d), lambda i, e: (i, 0)),
                  pl.BlockSpec((1, r, d), lambda i, e: (i // tiles_per_mod, 0, 0))],
        out_specs=pl.BlockSpec((tn, d), lambda i, e: (i, 0)),
        out_shape=jax.ShapeDtypeStruct((n, d), F32),
        scratch_shapes=[pltpu.VMEM((d, tn), F32), pltpu.VMEM((te, tn), F32), pltpu.VMEM((te, tn), BF16)],
        compiler_params=_cparams(2),
        name="peer_dense",
    )(h2, u_bf, vt_bf, e1, c1, packed(e2), packed(r2), x1, gf)


PEER_EXPERT_TILE = 1024
PEER_TOKEN_TILE = 512
PROJ_TOKEN_TILE = 512


def _layer_group(x, mods, per_token_mods, p, attend, conv_buf, s0, tm_proj, tm, tn_peer):
    b, t, d = x.shape
    n = b * t
    tiles = (tm_proj, tm, tn_peer)
    assert all(n % x_ == 0 for x_ in tiles) and (per_token_mods or all(t % x_ == 0 for x_ in tiles))
    sh_a, sc_a, g_a, sh_f, sc_f, g_f = mods
    tiles_per_seq = t // tm_proj if not per_token_mods else 1
    tiles_per_mod = n // tm_proj if per_token_mods else tiles_per_seq
    xf = x.reshape(n, d)
    q, k, v, kt, vt, lin, z, ba, km = _inproj(xf, sc_a, sh_a, p["norm_attn"], p["w_in"], p["qg"], p["kg"],
                                              p["bd"], tm_proj, tiles_per_mod, tiles_per_seq)
    o_attn = attend(q.reshape(b, t, ATTN_W), k.reshape(b, t, KV_W), v.reshape(b, t, KV_W), vt, km)
    tp = _round_up(t, GDN_CHUNK)
    pad3 = lambda a: jnp.pad(a.reshape(b, t, -1), ((0, 0), (0, tp - t), (0, 0)))
    cbuf8 = jnp.pad(conv_buf, ((0, 0), (SUBLANES - (CONV_W - 1), 0), (0, 0)))
    o_lin, s_new, conv_new = _gdn(pad3(lin), pad3(z), pad3(ba), p["conv_w"], cbuf8, s0,
                                  p["alog"], p["dtb"], p["og"], p["bd"], t)
    o_lin = o_lin[:, :t].reshape(n, LIN_W)
    x1, h2, qt = _outproj(o_attn.reshape(n, ATTN_W), o_lin, xf, g_a, sc_f, sh_f, p["norm_ffn"],
                          p["w_out"], p["wqt"], tm_proj, tiles_per_mod)
    e1, c1, e2, r2 = _peer_route(qt, p["peer_keys"], tm)
    x2 = _peer_dense(h2, p["u_bf"], p["vt_bf"], e1, c1, e2, r2, x1, g_f, tn_peer, PEER_EXPERT_TILE,
                     n // tn_peer if per_token_mods else t // tn_peer)
    to_rows = lambda a: a.reshape(-1, N_KV_HEADS, HEAD_DIM, a.shape[-1]).transpose(0, 3, 1, 2)
    return (x2.reshape(b, t, d), to_rows(kt).reshape(b, t, N_KV_HEADS, HEAD_DIM),
            to_rows(vt).reshape(b, t, N_KV_HEADS, HEAD_DIM), s_new, conv_new)


def kernel(x_prompt, x_sample, cache_k, cache_v, state_ssm, state_conv, page_table, c_prompt, c_sample,
           rel_bias, w_ada, b_ada, norm_attn, norm_ffn, w_in, w_out, q_norm, k_norm, conv_w, a_log,
           dt_bias, o_norm, peer_wq, peer_keys, peer_u, peer_v):
    depth = w_ada.shape[0]
    b, t, d = x_prompt.shape
    b2, s, _ = x_sample.shape
    n_pool = cache_k.shape[1]
    n_pages = page_table.shape[1]
    past = n_pages * PAGE_SIZE
    n_blocks = past // MOBA_BLOCK
    s_pad = 16
    assert t % MOBA_BLOCK == 0 and past % MOBA_BLOCK == 0 and n_pages % PAGES_PER_STEP == 0
    assert MOBA_BLOCK + 1 >= MAX_DISTANCE and s <= s_pad and n_blocks <= LANES
    assert w_in.shape[2] + LANES - 2 * H_LIN == D_IN_PAD

    mod = _ada(jnp.concatenate([c_prompt, c_sample], axis=0), w_ada, b_ada)
    btab = _relbias(rel_bias)
    rb_flat = rel_bias.reshape(-1)

    pool_view = lambda c: c.transpose(0, 1, 3, 4, 2).reshape(depth * n_pool, KV_W, PAGE_SIZE)
    ckt, cvt = pool_view(cache_k), pool_view(cache_v)
    bd = (jnp.arange(ATTN_W)[:, None] // HEAD_DIM == jnp.arange(ATTN_W)[None, :] // HEAD_DIM).astype(BF16)

    rows = N_HEADS * s
    bias_last = btab[1, :, :, :s].transpose(0, 2, 1).reshape(rows, MOBA_BLOCK)
    own = btab[0, :, :s_pad, :s].transpose(0, 2, 1)
    causal = jnp.arange(s_pad)[None, :] <= jnp.arange(s)[:, None]
    bias_own = jnp.where(causal[None], own, NEG_INF).reshape(rows, s_pad)
    far = jnp.repeat(rel_bias[N_BUCKETS - 1], s).reshape(rows, 1)
    kv_of_row = jnp.arange(rows) // (s * GQA)
    lane_kv = jnp.arange(KV_W) // HEAD_DIM
    row_mask = (kv_of_row[:, None] == lane_kv[None, :]).astype(F32)
    nbp = _round_up(t // MOBA_BLOCK, SUBLANES)

    xp, xs = x_prompt, x_sample
    outs = [[] for _ in range(8)]
    for l in range(depth):
        lane_pad = lambda a: jnp.pad(a[l][None, :], ((0, 0), (H_LIN, LANES - 2 * H_LIN)))
        p = {
            "norm_attn": norm_attn[l][None, :], "norm_ffn": norm_ffn[l][None, :],
            "w_in": jnp.pad(w_in[l], ((0, 0), (0, D_IN_PAD - w_in.shape[2]))).astype(BF16),
            "w_out": w_out[l].astype(BF16),
            "qg": jnp.tile(q_norm[l], N_HEADS)[None, :], "kg": jnp.tile(k_norm[l], N_KV_HEADS)[None, :],
            "bd": bd, "conv_w": conv_w[l], "alog": lane_pad(a_log), "dtb": lane_pad(dt_bias),
            "og": o_norm[l][None, :], "wqt": peer_wq[l].T.astype(BF16), "peer_keys": peer_keys[l],
            "u_bf": peer_u[l].astype(BF16), "vt_bf": peer_v[l].T.astype(BF16),
        }
        m6 = jnp.split(mod[l], 6, axis=-1)
        mods_p = [m[:b][:, None, :] for m in m6]
        mods_s = [jnp.repeat(m[b:], s, axis=0)[None] for m in m6]

        def attend_p(q, k, v, vt, km):
            kmean = jnp.pad(km.reshape(b, t // MOBA_BLOCK, KV_W), ((0, 0), (0, nbp - t // MOBA_BLOCK), (0, 0)))
            return _moba_prompt(rb_flat, q, k, vt, kmean, btab)

        pt_flat = (page_table + l * n_pool).reshape(-1).astype(I32)

        def attend_s(q, k, v, vt, km):
            qr = q.reshape(b2, s, N_HEADS, HEAD_DIM).transpose(0, 2, 1, 3).reshape(b2, rows, HEAD_DIM)
            qw = jnp.tile(qr, (1, 1, N_KV_HEADS)) * row_mask[None]
            padn = lambda a: jnp.pad(a, ((0, 0), (0, s_pad - s), (0, 0)))
            o = _moba_sample(pt_flat, ckt, cvt, qw, padn(k), padn(v), bias_last, bias_own, far, n_pages)
            return o.reshape(b2, N_HEADS, s, HEAD_DIM).transpose(0, 2, 1, 3).reshape(b2, s, ATTN_W)

        conv0 = jnp.zeros((b, CONV_W - 1, CONV_CH), F32)
        ssm0 = jnp.zeros((b, H_LIN, HEAD_DIM, HEAD_DIM), F32)
        xp, kp, vp, sp, cp = _layer_group(xp, mods_p, False, p, attend_p, conv0, ssm0,
                                          PROJ_TOKEN_TILE, MOBA_BLOCK, PEER_TOKEN_TILE)
        xs, ks, vs, ss, cs = _layer_group(xs, mods_s, True, p, attend_s, state_conv[l], state_ssm[l],
                                          b2 * s, b2 * s, b2 * s)
        for lst, val in zip(outs, (kp, vp, ks, vs, sp, ss, cp, cs)):
            lst.append(val)
    return (xp, xs) + tuple(jnp.stack(o) for o in outs)
```
